```python
import math
import jax, jax.numpy as jnp
from jax import lax
import numpy as np

D_MODEL = 2048
BATCH = 16
SEQ = 256
DEPTH = 1
DEC_BATCH = 4
DEC_SEQ = 2048
PAST_LEN = 256

GRID_W = 64
N_HEADS = 16
N_KV_HEADS = 4
HEAD_DIM = 128
Q_GROUPS = N_HEADS // N_KV_HEADS
WINDOW = 128
BLOCK = 128
ROPE_THETA = 10000.0
FOURIER_GROUPS = 4
FOURIER_GROUP_DIM = 256
FOURIER_DIM = FOURIER_GROUPS * FOURIER_GROUP_DIM
N_EXPERTS = 16
EXPERT_FF = 2048
CAPACITY_FACTOR = 2
N_BRANCHES = 2
Q_DIM = N_HEADS * HEAD_DIM
KV_DIM = N_KV_HEADS * HEAD_DIM
IN_DIM = FOURIER_DIM + Q_DIM + 2 * KV_DIM + N_BRANCHES * D_MODEL
N_MOD = 6
EPS = 1e-6
NEG_INF = -1e30

kernel_name = "hybrid_fnet_swa_ec_diffusion_step"


def rmsnorm(x, g):
    xf = x.astype(jnp.float32)
    y = xf * lax.rsqrt(jnp.mean(xf * xf, axis=-1, keepdims=True) + EPS)
    return (y * g.astype(jnp.float32)).astype(x.dtype)


def ada_chunks(cvec, w_mod, b_mod):
    m = jax.nn.silu(cvec) @ w_mod + b_mod
    return jnp.split(m[..., None, :], N_MOD, axis=-1)


def rope_1d(x, pos):
    d = x.shape[-1]
    inv = ROPE_THETA ** (-jnp.arange(0, d, 2, dtype=jnp.float32) / d)
    ang = pos.astype(jnp.float32)[:, None] * inv[None, :]
    cos = jnp.cos(ang)[:, None, :]
    sin = jnp.sin(ang)[:, None, :]
    xf = x.astype(jnp.float32)
    x1, x2 = jnp.split(xf, 2, axis=-1)
    out = jnp.concatenate([x1 * cos - x2 * sin, x2 * cos + x1 * sin], axis=-1)
    return out.astype(x.dtype)


def axial_rope(x):
    S = x.shape[1]
    rows = S // GRID_W
    row = jnp.repeat(jnp.arange(rows, dtype=jnp.int32), GRID_W)
    col = jnp.tile(jnp.arange(GRID_W, dtype=jnp.int32), rows)
    half = HEAD_DIM // 2
    return jnp.concatenate([rope_1d(x[..., :half], row), rope_1d(x[..., half:], col)], axis=-1)


def mixer_inputs(x, norm_g, shift, scale, w_in):
    h = rmsnorm(x, norm_g) * (1 + scale) + shift
    z = h @ w_in
    o1 = FOURIER_DIM
    o2 = o1 + Q_DIM
    o3 = o2 + KV_DIM
    o4 = o3 + KV_DIM
    u_f, q, k, v, gl = jnp.split(z, [o1, o2, o3, o4], axis=-1)
    B, S = x.shape[0], x.shape[1]
    q = q.reshape(B, S, N_HEADS, HEAD_DIM)
    k = k.reshape(B, S, N_KV_HEADS, HEAD_DIM)
    v = v.reshape(B, S, N_KV_HEADS, HEAD_DIM)
    return u_f, q, k, v, gl


def fourier_mix(u):
    B, S, _ = u.shape
    uf = u.astype(jnp.float32).reshape(B, S, FOURIER_GROUPS, FOURIER_GROUP_DIM)
    z = jnp.fft.fft2(uf, axes=(1, 3), norm="ortho").real
    return z.reshape(B, S, FOURIER_DIM).astype(u.dtype)


def ctx_attention(q, k, v, sink):
    B, S = q.shape[0], q.shape[1]
    qg = q.reshape(B, S, N_KV_HEADS, Q_GROUPS, HEAD_DIM)
    s = jnp.einsum('bqhgd,bkhd->bhgqk', qg, k).astype(jnp.float32) * (HEAD_DIM ** -0.5)
    sk = jnp.broadcast_to(sink.astype(jnp.float32).reshape(1, N_KV_HEADS, Q_GROUPS, 1, 1),
                          (B, N_KV_HEADS, Q_GROUPS, S, 1))
    p = jax.nn.softmax(jnp.concatenate([s, sk], axis=-1), axis=-1)[..., :S]
    o = jnp.einsum('bhgqk,bkhd->bqhgd', p.astype(v.dtype), v)
    return o.reshape(B, S, Q_DIM)


def latent_attention(q, k, v, k_ctx, v_ctx, sink):
    B, S = q.shape[0], q.shape[1]
    C = k_ctx.shape[1]
    nb = S // BLOCK
    L = 3 * BLOCK
    qb = q.reshape(B, nb, BLOCK, N_KV_HEADS, Q_GROUPS, HEAD_DIM)
    pad = ((0, 0), (BLOCK, BLOCK), (0, 0), (0, 0))
    kp = jnp.pad(k, pad).reshape(B, nb + 2, BLOCK, N_KV_HEADS, HEAD_DIM)
    vp = jnp.pad(v, pad).reshape(B, nb + 2, BLOCK, N_KV_HEADS, HEAD_DIM)
    kw = jnp.concatenate([kp[:, :nb], kp[:, 1:nb + 1], kp[:, 2:nb + 2]], axis=2)
    vw = jnp.concatenate([vp[:, :nb], vp[:, 1:nb + 1], vp[:, 2:nb + 2]], axis=2)
    qpos = jnp.arange(S, dtype=jnp.int32).reshape(nb, BLOCK)
    kpos = (jnp.arange(nb, dtype=jnp.int32) * BLOCK)[:, None] - BLOCK + jnp.arange(L, dtype=jnp.int32)[None, :]
    kp3 = kpos[:, None, :]
    valid = (jnp.abs(qpos[:, :, None] - kp3) <= WINDOW) & (kp3 >= 0) & (kp3 < S)
    scale = HEAD_DIM ** -0.5
    s_loc = jnp.einsum('bnqhgd,bnkhd->bnhgqk', qb, kw).astype(jnp.float32) * scale
    s_loc = jnp.where(valid[None, :, None, None, :, :], s_loc, NEG_INF)
    s_ctx = jnp.einsum('bnqhgd,bchd->bnhgqc', qb, k_ctx).astype(jnp.float32) * scale
    sk = jnp.broadcast_to(sink.astype(jnp.float32).reshape(1, 1, N_KV_HEADS, Q_GROUPS, 1, 1),
                          (B, nb, N_KV_HEADS, Q_GROUPS, BLOCK, 1))
    p = jax.nn.softmax(jnp.concatenate([s_loc, s_ctx, sk], axis=-1), axis=-1).astype(v.dtype)
    o = (jnp.einsum('bnhgqk,bnkhd->bnqhgd', p[..., :L], vw)
         + jnp.einsum('bnhgqc,bchd->bnqhgd', p[..., L:L + C], v_ctx))
    return o.reshape(B, S, Q_DIM)


def merge_branches(u_f, attn, gl, w_fo, w_ao, w_out):
    f = fourier_mix(u_f) @ w_fo
    a = attn @ w_ao
    g_f, g_a = jnp.split(jax.nn.sigmoid(gl), N_BRANCHES, axis=-1)
    return (g_f * f + g_a * a) @ w_out


def ec_moe(h, w_router, w_gate, w_up, w_down):
    B, S, D = h.shape
    t = h.reshape(B * S, D)
    n = B * S
    cap = CAPACITY_FACTOR * n // N_EXPERTS
    aff = jax.nn.softmax((t @ w_router).astype(jnp.float32), axis=-1)
    g, idx = lax.top_k(aff.T, cap)
    xs = t[idx]
    a = jnp.einsum('ecd,edf->ecf', xs, w_gate)
    u = jnp.einsum('ecd,edf->ecf', xs, w_up)
    ye = jnp.einsum('ecf,efd->ecd', jax.nn.silu(a) * u, w_down) * g[..., None].astype(t.dtype)
    out = jnp.zeros_like(t).at[idx.reshape(-1)].add(ye.reshape(-1, D))
    return out.reshape(B, S, D)


def setup_inputs(seed: int = 0) -> dict:
    key = jax.random.key(seed)
    ks = jax.random.split(key, 24)
    f32 = jnp.float32
    nrm = lambda k, shp, s: jax.random.normal(k, shp, f32) * s
    return {
        "x_prompt": nrm(ks[0], (BATCH, SEQ, D_MODEL), 1.0),
        "x_sample": nrm(ks[1], (DEC_BATCH, DEC_SEQ, D_MODEL), 1.0),
        "cache_k": nrm(ks[2], (DEC_BATCH, DEPTH, PAST_LEN, N_KV_HEADS, HEAD_DIM), 1.0),
        "cache_v": nrm(ks[3], (DEC_BATCH, DEPTH, PAST_LEN, N_KV_HEADS, HEAD_DIM), 1.0),
        "c": nrm(ks[4], (DEC_BATCH, D_MODEL), 1.0),
        "c_ctx": nrm(ks[5], (D_MODEL,), 1.0),
        "w_mod": nrm(ks[6], (DEPTH, D_MODEL, N_MOD * D_MODEL), 0.5 * D_MODEL ** -0.5),
        "b_mod": nrm(ks[7], (DEPTH, N_MOD * D_MODEL), 0.02),
        "norm_mix": 1.0 + nrm(ks[8], (DEPTH, D_MODEL), 0.02),
        "w_in": nrm(ks[9], (DEPTH, D_MODEL, IN_DIM), D_MODEL ** -0.5),
        "attn_sink": nrm(ks[10], (DEPTH, N_HEADS), 0.5),
        "w_fourier_out": nrm(ks[11], (DEPTH, FOURIER_DIM, D_MODEL), FOURIER_DIM ** -0.5),
        "w_attn_out": nrm(ks[12], (DEPTH, Q_DIM, D_MODEL), Q_DIM ** -0.5),
        "w_merge_out": nrm(ks[13], (DEPTH, D_MODEL, D_MODEL), D_MODEL ** -0.5),
        "norm_ffn": 1.0 + nrm(ks[14], (DEPTH, D_MODEL), 0.02),
        "w_router": nrm(ks[15], (DEPTH, D_MODEL, N_EXPERTS), D_MODEL ** -0.5),
        "w_exp_gate": nrm(ks[16], (DEPTH, N_EXPERTS, D_MODEL, EXPERT_FF), D_MODEL ** -0.5),
        "w_exp_up": nrm(ks[17], (DEPTH, N_EXPERTS, D_MODEL, EXPERT_FF), D_MODEL ** -0.5),
        "w_exp_down": nrm(ks[18], (DEPTH, N_EXPERTS, EXPERT_FF, D_MODEL), EXPERT_FF ** -0.5),
        "final_norm": 1.0 + nrm(ks[19], (D_MODEL,), 0.02),
    }


def reference(x_prompt, x_sample, cache_k, cache_v, c, c_ctx, w_mod, b_mod, norm_mix, w_in, attn_sink,
              w_fourier_out, w_attn_out, w_merge_out, norm_ffn, w_router, w_exp_gate, w_exp_up,
              w_exp_down, final_norm):
    xp = x_prompt
    xs = x_sample
    ks_out = []
    vs_out = []
    for l in range(DEPTH):
        sh1, sc1, g1, sh2, sc2, g2 = ada_chunks(c_ctx, w_mod[l], b_mod[l])
        u_f, q, k, v, gl = mixer_inputs(xp, norm_mix[l], sh1, sc1, w_in[l])
        attn = ctx_attention(q, k, v, attn_sink[l])
        xp = xp + g1 * merge_branches(u_f, attn, gl, w_fourier_out[l], w_attn_out[l], w_merge_out[l])
        h2 = rmsnorm(xp, norm_ffn[l]) * (1 + sc2) + sh2
        xp = xp + g2 * ec_moe(h2, w_router[l], w_exp_gate[l], w_exp_up[l], w_exp_down[l])
        ks_out.append(k)
        vs_out.append(v)

        sh1, sc1, g1, sh2, sc2, g2 = ada_chunks(c, w_mod[l], b_mod[l])
        u_f, q, k, v, gl = mixer_inputs(xs, norm_mix[l], sh1, sc1, w_in[l])
        q = axial_rope(q)
        k = axial_rope(k)
        attn = latent_attention(q, k, v, cache_k[:, l], cache_v[:, l], attn_sink[l])
        xs = xs + g1 * merge_branches(u_f, attn, gl, w_fourier_out[l], w_attn_out[l], w_merge_out[l])
        h2 = rmsnorm(xs, norm_ffn[l]) * (1 + sc2) + sh2
        xs = xs + g2 * ec_moe(h2, w_router[l], w_exp_gate[l], w_exp_up[l], w_exp_down[l])
    y_prompt = rmsnorm(xp, final_norm)
    y_sample = rmsnorm(xs, final_norm)
    state_k = jnp.stack(ks_out, axis=1)
    state_v = jnp.stack(vs_out, axis=1)
    return (y_prompt, y_sample, state_k, state_v)
```

```python
import functools
import math

import numpy as np
import jax
import jax.numpy as jnp
from jax import lax
from jax.experimental import pallas as pl
from jax.experimental.pallas import tpu as pltpu

D_MODEL = 2048
BATCH = 16
SEQ = 256
DEC_BATCH = 4
DEC_SEQ = 2048
PAST_LEN = 256
GRID_W = 64
N_HEADS = 16
N_KV_HEADS = 4
HEAD_DIM = 128
Q_GROUPS = N_HEADS // N_KV_HEADS
WINDOW = 128
ROPE_THETA = 10000.0
FOURIER_GROUPS = 4
FOURIER_GROUP_DIM = 256
FOURIER_DIM = FOURIER_GROUPS * FOURIER_GROUP_DIM
N_EXPERTS = 16
EXPERT_FF = 2048
CAPACITY_FACTOR = 2
Q_DIM = N_HEADS * HEAD_DIM
KV_DIM = N_KV_HEADS * HEAD_DIM
IN_DIM = FOURIER_DIM + Q_DIM + 2 * KV_DIM + 2 * D_MODEL
N_MOD = 6
EPS = 1e-6
NEG_INF = -1e30

F32 = jnp.float32
BF16 = jnp.bfloat16
LANES = 128
VMEM_LIMIT = 56 * 1024 * 1024
N_CTX = BATCH * SEQ
N_LAT = DEC_BATCH * DEC_SEQ
CAP_CTX = CAPACITY_FACTOR * N_CTX // N_EXPERTS
CAP_LAT = CAPACITY_FACTOR * N_LAT // N_EXPERTS
CAP_ALL = CAP_CTX + CAP_LAT


def _cparams(*sem, row_dmas=False):
    return pltpu.CompilerParams(dimension_semantics=sem, vmem_limit_bytes=VMEM_LIMIT,
                                disable_bounds_checks=row_dmas)


def _split_bf16(x):
    hi = x.astype(BF16)
    lo = (x - hi.astype(F32)).astype(BF16)
    return hi, lo


def _ada_body(cv_ref, w_ref, b_ref, o_ref):
    cv = cv_ref[...]
    s = cv * jax.nn.sigmoid(cv)
    s_hi, s_lo = _split_bf16(s)
    w_hi, w_lo = _split_bf16(w_ref[...])
    acc = jnp.dot(s_hi, w_hi, preferred_element_type=F32)
    acc += jnp.dot(s_hi, w_lo, preferred_element_type=F32)
    acc += jnp.dot(s_lo, w_hi, preferred_element_type=F32)
    o_ref[...] = acc + b_ref[...]


def _ada(cv, w_mod, b_mod):
    rows = cv.shape[0]
    tn = 1024
    return pl.pallas_call(
        _ada_body,
        grid=(N_MOD * D_MODEL // tn,),
        in_specs=[pl.BlockSpec((rows, D_MODEL), lambda j: (0, 0)),
                  pl.BlockSpec((D_MODEL, tn), lambda j: (0, j)),
                  pl.BlockSpec((1, tn), lambda j: (0, j))],
        out_specs=pl.BlockSpec((rows, tn), lambda j: (0, j)),
        out_shape=jax.ShapeDtypeStruct((rows, N_MOD * D_MODEL), F32),
        compiler_params=_cparams("arbitrary"),
        name="ada",
    )(cv, w_mod, b_mod.reshape(1, -1))


def _rmsnorm_mod(x, gain, shift, scale):
    ms = jnp.mean(x * x, axis=-1, keepdims=True)
    y = x * lax.rsqrt(ms + EPS) * gain
    return y * (1.0 + scale) + shift


def _rope_chunk(c, cos, sin_signed, low):
    partner = jnp.where(low, pltpu.roll(c, 96, 1), pltpu.roll(c, 32, 1))
    return c * cos + partner * sin_signed


def _inproj_body(*refs, rope, tn):
    if rope:
        x_ref, mod_ref, gn_ref, w_ref, cos_ref, sin_ref, z_ref, h_scr = refs
    else:
        x_ref, mod_ref, gn_ref, w_ref, z_ref, kv_ref, h_scr = refs
    j = pl.program_id(1)

    @pl.when(j == 0)
    def _():
        h = _rmsnorm_mod(x_ref[...], gn_ref[...], mod_ref[0, 0:1, :], mod_ref[0, 1:2, :])
        h_scr[...] = h.astype(BF16)

    acc = jnp.dot(h_scr[...], w_ref[...], preferred_element_type=F32)
    heads_per_tile = tn // HEAD_DIM
    q_lo, q_hi = FOURIER_DIM // tn, (FOURIER_DIM + Q_DIM) // tn

    if not rope:
        z_ref[...] = acc.astype(BF16)

        @pl.when(j == q_hi)
        def _():
            kv_ref[...] = acc
    else:
        def roped(nheads):
            cos = cos_ref[...]
            sin = sin_ref[...]
            low = (lax.broadcasted_iota(jnp.int32, cos.shape, 1) & 32) == 0
            for hd in range(nheads):
                sl = slice(hd * HEAD_DIM, (hd + 1) * HEAD_DIM)
                z_ref[:, sl] = _rope_chunk(acc[:, sl], cos, sin, low).astype(BF16)

        @pl.when((j < q_lo) | (j > q_hi))
        def _():
            z_ref[...] = acc.astype(BF16)

        @pl.when((j >= q_lo) & (j < q_hi))
        def _():
            roped(heads_per_tile)

        @pl.when(j == q_hi)
        def _():
            roped(N_KV_HEADS)
            z_ref[:, KV_DIM:] = acc[:, KV_DIM:].astype(BF16)


def _inproj(x, mod, gain, w_in_b, tokens_per_mod, rope_tabs=None):
    n = x.shape[0]
    tm, tn = 1024, 1024
    rope = rope_tabs is not None
    in_specs = [pl.BlockSpec((tm, D_MODEL), lambda i, j: (i, 0)),
                pl.BlockSpec((1, N_MOD, D_MODEL), lambda i, j: (i * tm // tokens_per_mod, 0, 0)),
                pl.BlockSpec((1, D_MODEL), lambda i, j: (0, 0)),
                pl.BlockSpec((D_MODEL, tn), lambda i, j: (0, j))]
    args = [x, mod, gain, w_in_b]
    z_spec = pl.BlockSpec((tm, tn), lambda i, j: (i, j))
    z_shape = jax.ShapeDtypeStruct((n, IN_DIM), BF16)
    if rope:
        tiles_per_seq = DEC_SEQ // tm
        in_specs += [pl.BlockSpec((tm, HEAD_DIM), lambda i, j: (i % tiles_per_seq, 0))] * 2
        args += list(rope_tabs)
        out_specs, out_shape = z_spec, z_shape
    else:
        out_specs = [z_spec, pl.BlockSpec((tm, 2 * KV_DIM), lambda i, j: (i, 0))]
        out_shape = [z_shape, jax.ShapeDtypeStruct((n, 2 * KV_DIM), F32)]
    return pl.pallas_call(
        functools.partial(_inproj_body, rope=rope, tn=tn),
        grid=(n // tm, IN_DIM // tn),
        in_specs=in_specs, out_specs=out_specs, out_shape=out_shape,
        scratch_shapes=[pltpu.VMEM((tm, D_MODEL), BF16)],
        compiler_params=_cparams("arbitrary", "arbitrary"),
        name="inproj_lat" if rope else "inproj_ctx",
    )(*args)


def _rope_tables():
    d = np.arange(HEAD_DIM)
    half, dd = d // 64, d % 64
    inv = ROPE_THETA ** (-jnp.arange(0, 64, 2, dtype=F32) / 64)
    inv_d = inv[jnp.asarray(dd % 32)]
    s = np.arange(DEC_SEQ)
    pos = np.where(half[None, :] == 0, (s // GRID_W)[:, None], (s % GRID_W)[:, None])
    ang = jnp.asarray(pos, F32) * inv_d[None, :]
    sign = jnp.asarray(np.where(dd < 32, -1.0, 1.0), F32)
    return jnp.cos(ang), jnp.sin(ang) * sign[None, :]


def _fourier_body(u_ref, cc_ref, cs_ref, o_ref, pq_scr, *, seq):
    @pl.when(pl.program_id(1) == 0)
    def _():
        for g in range(FOURIER_GROUPS):
            sl = slice(g * FOURIER_GROUP_DIM, (g + 1) * FOURIER_GROUP_DIM)
            pq = jnp.dot(u_ref[:, sl], cc_ref[...], preferred_element_type=F32)
            pq_scr[0:seq, sl] = pq[:, :FOURIER_GROUP_DIM].astype(BF16)
            pq_scr[seq:2 * seq, sl] = pq[:, FOURIER_GROUP_DIM:].astype(BF16)

    o_ref[...] = jnp.dot(cs_ref[...], pq_scr[...], preferred_element_type=F32).astype(BF16)


def _dft_mats(n):
    k = np.arange(n)
    ang = 2.0 * np.pi * ((k[:, None] * k[None, :]) % n) / n
    return np.cos(ang) / math.sqrt(n), np.sin(ang) / math.sqrt(n)


def _fourier(z, seq):
    n = z.shape[0]
    tr = min(seq, 512)
    cc, sc = _dft_mats(FOURIER_GROUP_DIM)
    cs, ss = _dft_mats(seq)
    chan = jnp.asarray(np.concatenate([cc, sc], axis=1), F32).astype(BF16)
    posm = jnp.asarray(np.concatenate([cs, -ss], axis=1), F32).astype(BF16)
    tiles = seq // tr
    return pl.pallas_call(
        functools.partial(_fourier_body, seq=seq),
        grid=(n // seq, tiles),
        in_specs=[pl.BlockSpec((seq, FOURIER_DIM), lambda b, i: (b, 0)),
                  pl.BlockSpec((FOURIER_GROUP_DIM, 2 * FOURIER_GROUP_DIM), lambda b, i: (0, 0)),
                  pl.BlockSpec((tr, 2 * seq), lambda b, i: (i, 0))],
        out_specs=pl.BlockSpec((tr, FOURIER_DIM), lambda b, i: (b * tiles + i, 0)),
        out_shape=jax.ShapeDtypeStruct((n, FOURIER_DIM), BF16),
        scratch_shapes=[pltpu.VMEM((2 * seq, FOURIER_DIM), BF16)],
        compiler_params=_cparams("arbitrary", "arbitrary"),
        name=f"fourier_{seq}",
    )(z, chan, posm)


_NT = (((1,), (1,)), ((), ()))
_SCALE = HEAD_DIM ** -0.5


def _attn_ctx_body(sink_ref, q_ref, k_ref, v_ref, o_ref):
    h = pl.program_id(1)
    k = k_ref[...]
    v = v_ref[...]
    for g in range(Q_GROUPS):
        sl = slice(g * HEAD_DIM, (g + 1) * HEAD_DIM)
        sk = sink_ref[0, h * Q_GROUPS + g]
        s = lax.dot_general(q_ref[:, sl], k, _NT, preferred_element_type=F32) * _SCALE
        m = jnp.maximum(jnp.max(s, axis=-1, keepdims=True), sk)
        p = jnp.exp(s - m)
        denom = jnp.sum(p, axis=-1, keepdims=True) + jnp.exp(sk - m)
        o = jnp.dot(p.astype(BF16), v, preferred_element_type=F32) / denom
        o_ref[:, sl] = o.astype(BF16)


def _attn_ctx(z, sink):
    n = z.shape[0]
    gw = Q_GROUPS * HEAD_DIM
    q0 = FOURIER_DIM // gw
    k0 = (FOURIER_DIM + Q_DIM) // HEAD_DIM
    v0 = k0 + N_KV_HEADS
    return pl.pallas_call(
        _attn_ctx_body,
        grid=(n // SEQ, N_KV_HEADS),
        in_specs=[pl.BlockSpec(memory_space=pltpu.SMEM),
                  pl.BlockSpec((SEQ, gw), lambda b, h: (b, q0 + h)),
                  pl.BlockSpec((SEQ, HEAD_DIM), lambda b, h: (b, k0 + h)),
                  pl.BlockSpec((SEQ, HEAD_DIM), lambda b, h: (b, v0 + h))],
        out_specs=pl.BlockSpec((SEQ, gw), lambda b, h: (b, h)),
        out_shape=jax.ShapeDtypeStruct((n, Q_DIM), BF16),
        compiler_params=_cparams("arbitrary", "arbitrary"),
        name="attn_ctx",
    )(sink, z, z, z)


ATT_TQ = 256
ATT_WIN = ATT_TQ + 2 * WINDOW


def _attn_lat_body(sink_ref, q_ref, k_ref, v_ref, ck_ref, cv_ref, o_ref):
    h = pl.program_id(1)
    nq = pl.program_id(2)
    start = jnp.clip(nq * ATT_TQ - WINDOW, 0, DEC_SEQ - ATT_WIN)
    start = pl.multiple_of(start, WINDOW)
    kl = k_ref[pl.ds(start, ATT_WIN), :]
    vl = v_ref[pl.ds(start, ATT_WIN), :]
    kc = ck_ref[0].astype(BF16)
    vc = cv_ref[0].astype(BF16)
    qpos = nq * ATT_TQ + lax.broadcasted_iota(jnp.int32, (ATT_TQ, ATT_WIN), 0)
    kpos = start + lax.broadcasted_iota(jnp.int32, (ATT_TQ, ATT_WIN), 1)
    valid = jnp.abs(qpos - kpos) <= WINDOW
    for g in range(Q_GROUPS):
        sl = slice(g * HEAD_DIM, (g + 1) * HEAD_DIM)
        sk = sink_ref[0, h * Q_GROUPS + g]
        q = q_ref[:, sl]
        s_loc = lax.dot_general(q, kl, _NT, preferred_element_type=F32) * _SCALE
        s_loc = jnp.where(valid, s_loc, NEG_INF)
        s_ctx = lax.dot_general(q, kc, _NT, preferred_element_type=F32) * _SCALE
        m = jnp.maximum(jnp.maximum(jnp.max(s_loc, axis=-1, keepdims=True),
                                    jnp.max(s_ctx, axis=-1, keepdims=True)), sk)
        p_loc = jnp.exp(s_loc - m)
        p_ctx = jnp.exp(s_ctx - m)
        denom = (jnp.sum(p_loc, axis=-1, keepdims=True) + jnp.sum(p_ctx, axis=-1, keepdims=True)
                 + jnp.exp(sk - m))
        o = (jnp.dot(p_loc.astype(BF16), vl, preferred_element_type=F32)
             + jnp.dot(p_ctx.astype(BF16), vc, preferred_element_type=F32)) / denom
        o_ref[:, sl] = o.astype(BF16)


def _attn_lat(z, cache_k, cache_v, sink):
    n = z.shape[0]
    gw = Q_GROUPS * HEAD_DIM
    q0 = FOURIER_DIM // gw
    k0 = (FOURIER_DIM + Q_DIM) // HEAD_DIM
    v0 = k0 + N_KV_HEADS
    tiles = DEC_SEQ // ATT_TQ
    return pl.pallas_call(
        _attn_lat_body,
        grid=(n // DEC_SEQ, N_KV_HEADS, tiles),
        in_specs=[pl.BlockSpec(memory_space=pltpu.SMEM),
                  pl.BlockSpec((ATT_TQ, gw), lambda b, h, t: (b * tiles + t, q0 + h)),
                  pl.BlockSpec((DEC_SEQ, HEAD_DIM), lambda b, h, t: (b, k0 + h)),
                  pl.BlockSpec((DEC_SEQ, HEAD_DIM), lambda b, h, t: (b, v0 + h)),
                  pl.BlockSpec((1, PAST_LEN, HEAD_DIM), lambda b, h, t: (b, 0, h)),
                  pl.BlockSpec((1, PAST_LEN, HEAD_DIM), lambda b, h, t: (b, 0, h))],
        out_specs=pl.BlockSpec((ATT_TQ, gw), lambda b, h, t: (b * tiles + t, h)),
        out_shape=jax.ShapeDtypeStruct((n, Q_DIM), BF16),
        compiler_params=_cparams("arbitrary", "arbitrary", "arbitrary"),
        name="attn_lat",
    )(sink, z, z, z, cache_k, cache_v)


def _merge1_body(f_ref, a_ref, gf_ref, ga_ref, wf_ref, wa_ref, o_ref):
    f = jnp.dot(f_ref[...], wf_ref[...], preferred_element_type=F32)
    a = jnp.dot(a_ref[...], wa_ref[...], preferred_element_type=F32)
    gf = jax.nn.sigmoid(gf_ref[...].astype(F32))
    ga = jax.nn.sigmoid(ga_ref[...].astype(F32))
    o_ref[...] = (gf * f + ga * a).astype(BF16)


def _merge1(fmix, attn, z, w_fo_b, w_ao_b):
    n = fmix.shape[0]
    tm = 512
    g0 = (FOURIER_DIM + Q_DIM + 2 * KV_DIM) // D_MODEL
    return pl.pallas_call(
        _merge1_body,
        grid=(n // tm,),
        in_specs=[pl.BlockSpec((tm, FOURIER_DIM), lambda i: (i, 0)),
                  pl.BlockSpec((tm, Q_DIM), lambda i: (i, 0)),
                  pl.BlockSpec((tm, D_MODEL), lambda i: (i, g0)),
                  pl.BlockSpec((tm, D_MODEL), lambda i: (i, g0 + 1)),
                  pl.BlockSpec((FOURIER_DIM, D_MODEL), lambda i: (0, 0)),
                  pl.BlockSpec((Q_DIM, D_MODEL), lambda i: (0, 0))],
        out_specs=pl.BlockSpec((tm, D_MODEL), lambda i: (i, 0)),
        out_shape=jax.ShapeDtypeStruct((n, D_MODEL), BF16),
        compiler_params=_cparams("arbitrary"),
        name="merge1",
    )(fmix, attn, z, z, w_fo_b, w_ao_b)


def _merge2_body(m_ref, x_ref, mod_ref, gn_ref, wo_ref, wr_ref, x1_ref, hp_ref, aff_ref):
    out = jnp.dot(m_ref[...], wo_ref[...], preferred_element_type=F32)
    x1 = x_ref[...] + mod_ref[0, 2:3, :] * out
    x1_ref[...] = x1
    h2 = _rmsnorm_mod(x1, gn_ref[...], mod_ref[0, 3:4, :], mod_ref[0, 4:5, :])
    hb = h2.astype(BF16)
    hp_ref[...] = hb.astype(F32)
    logits = lax.dot_general(wr_ref[...], hb, _NT, preferred_element_type=F32)
    mx = jnp.max(logits, axis=0, keepdims=True)
    ex = jnp.exp(logits - mx)
    aff_ref[...] = ex / jnp.sum(ex, axis=0, keepdims=True)


def _merge2(mrg, x, mod, gain, w_out_b, w_rt_b, tokens_per_mod):
    n = x.shape[0]
    tm = 512
    return pl.pallas_call(
        _merge2_body,
        grid=(n // tm,),
        in_specs=[pl.BlockSpec((tm, D_MODEL), lambda i: (i, 0)),
                  pl.BlockSpec((tm, D_MODEL), lambda i: (i, 0)),
                  pl.BlockSpec((1, N_MOD, D_MODEL), lambda i: (i * tm // tokens_per_mod, 0, 0)),
                  pl.BlockSpec((1, D_MODEL), lambda i: (0, 0)),
                  pl.BlockSpec((D_MODEL, D_MODEL), lambda i: (0, 0)),
                  pl.BlockSpec((N_EXPERTS, D_MODEL), lambda i: (0, 0))],
        out_specs=[pl.BlockSpec((tm, D_MODEL), lambda i: (i, 0)),
                   pl.BlockSpec((tm, D_MODEL), lambda i: (i, 0)),
                   pl.BlockSpec((N_EXPERTS, tm), lambda i: (0, i))],
        out_shape=[jax.ShapeDtypeStruct((n, D_MODEL), F32),
                   jax.ShapeDtypeStruct((n, D_MODEL), F32),
                   jax.ShapeDtypeStruct((N_EXPERTS, n), F32)],
        compiler_params=_cparams("arbitrary"),
        name="merge2",
    )(mrg, x, mod, gain, w_out_b, w_rt_b)


ROUTE_JB = 256
COL_TOK = 0
COL_GATE = 8


def _cumsum_lanes(dst_ref, mask, n):
    x = jnp.where(mask, 1.0, 0.0).astype(BF16)
    tri = (lax.broadcasted_iota(jnp.int32, (LANES, LANES), 0)
           <= lax.broadcasted_iota(jnp.int32, (LANES, LANES), 1)).astype(BF16)
    off = jnp.zeros((N_EXPERTS, 1), F32)
    for c in range(n // LANES):
        sl = slice(c * LANES, (c + 1) * LANES)
        pc = jnp.dot(x[:, sl], tri, preferred_element_type=F32) + off
        dst_ref[:, sl] = pc
        off = pc[:, LANES - 1:LANES]


def _route_body(aff_ref, res_ref, key_scr, m_scr, r_scr, *, n, cap):
    e = pl.program_id(0)

    @pl.when(e == 0)
    def _():
        a = aff_ref[...]
        def search(k, thr):
            cand = thr | lax.shift_left(jnp.int32(1), 30 - k)
            cnt = jnp.sum(jnp.where(a >= lax.bitcast_convert_type(cand, F32), 1.0, 0.0), axis=1, keepdims=True)
            return jnp.where(cnt >= cap, cand, thr)

        thr_bits = lax.fori_loop(0, 31, search, jnp.zeros((N_EXPERTS, 1), jnp.int32))
        thr = lax.bitcast_convert_type(thr_bits, F32)
        gt = a > thr
        eq = a == thr
        need = cap - jnp.sum(jnp.where(gt, 1.0, 0.0), axis=1, keepdims=True)
        _cumsum_lanes(key_scr, eq, n)
        sel = gt | (eq & (key_scr[...] <= need))
        _cumsum_lanes(key_scr, sel, n)
        key_scr[...] = jnp.where(sel, key_scr[...] - 1.0, -1.0)

        tok = lax.broadcasted_iota(jnp.int32, (8, n), 1)
        row = lax.broadcasted_iota(jnp.int32, (8, n), 0)
        m_scr[0:8, :] = jnp.where(row == 0, tok // LANES, jnp.where(row == 1, tok % LANES, 0)).astype(F32)
        a_hi = a.astype(BF16).astype(F32)
        a_mid = (a - a_hi).astype(BF16).astype(F32)
        a_lo = (a - a_hi - a_mid).astype(BF16).astype(F32)
        m_scr[COL_GATE:COL_GATE + N_EXPERTS, :] = a_hi
        m_scr[COL_GATE + N_EXPERTS:COL_GATE + 2 * N_EXPERTS, :] = a_mid
        m_scr[COL_GATE + 2 * N_EXPERTS:COL_GATE + 3 * N_EXPERTS, :] = a_lo
        m_scr[COL_GATE + 3 * N_EXPERTS:, :] = jnp.zeros((LANES - COL_GATE - 3 * N_EXPERTS, n), F32)
        for c in range(n // LANES):
            sl = slice(c * LANES, (c + 1) * LANES)
            r_scr[sl, :] = m_scr[:, sl].T.astype(BF16)

    key = key_scr[pl.ds(e, 1), :]
    for jb in range(cap // ROUTE_JB):
        slot = (lax.broadcasted_iota(jnp.int32, (ROUTE_JB, 1), 0) + jb * ROUTE_JB).astype(F32)
        onehot = jnp.where(key == slot, 1.0, 0.0).astype(BF16)
        res_ref[0, jb * ROUTE_JB:(jb + 1) * ROUTE_JB, :] = jnp.dot(onehot, r_scr[...], preferred_element_type=F32)


def _route(aff_t, cap):
    n = aff_t.shape[1]
    res = pl.pallas_call(
        functools.partial(_route_body, n=n, cap=cap),
        grid=(N_EXPERTS,),
        in_specs=[pl.BlockSpec((N_EXPERTS, n), lambda e: (0, 0))],
        out_specs=pl.BlockSpec((1, cap, LANES), lambda e: (e, 0, 0)),
        out_shape=jax.ShapeDtypeStruct((N_EXPERTS, cap, LANES), F32),
        scratch_shapes=[pltpu.VMEM((N_EXPERTS, n), F32),
                        pltpu.VMEM((LANES, n), F32),
                        pltpu.VMEM((n, LANES), BF16)],
        compiler_params=_cparams("arbitrary"),
        name=f"route_{n}",
    )(aff_t)
    idx = (res[:, :, COL_TOK] * LANES + res[:, :, COL_TOK + 1]).astype(jnp.int32)
    ar = jnp.arange(N_EXPERTS)
    gate = (res[ar, :, COL_GATE + ar] + res[ar, :, COL_GATE + N_EXPERTS + ar]
            + res[ar, :, COL_GATE + 2 * N_EXPERTS + ar])
    return idx, gate


MOE_TF = 512
MOE_STEPS = EXPERT_FF // MOE_TF
PIECE = CAP_ALL // MOE_STEPS


def _piece_parts(q):
    lo, hi = q * PIECE, (q + 1) * PIECE
    parts = []
    if lo < CAP_CTX:
        parts.append((0, lo, min(hi, CAP_CTX)))
    if hi > CAP_CTX:
        parts.append((1, max(lo, CAP_CTX) - CAP_CTX, hi - CAP_CTX))
    return parts


def _moe1_body(idxc_ref, idxl_ref, hc_ref, hl_ref, wg_ref, wu_ref, o_ref, stage, xb, sems):
    e = pl.program_id(0)
    f = pl.program_id(1)
    nxt = e + 1
    idx_refs, h_refs = (idxc_ref, idxl_ref), (hc_ref, hl_ref)

    def issue(expert, q, slot):
        for grp, first, last in _piece_parts(q):
            off = (CAP_CTX if grp else 0) - q * PIECE

            def body(j, carry, grp=grp, off=off):
                t = idx_refs[grp][expert, j]
                pltpu.make_async_copy(h_refs[grp].at[pl.ds(t, 1), :], stage.at[slot, pl.ds(j + off, 1), :],
                                      sems.at[slot]).start()
                return carry

            lax.fori_loop(first, last, body, 0, unroll=8)

    def consume(q, slot, dst):
        pltpu.make_async_copy(hl_ref.at[pl.ds(0, PIECE), :], stage.at[slot], sems.at[slot]).wait()
        xb[dst, q * PIECE:(q + 1) * PIECE, :] = stage[slot].astype(BF16)

    @pl.when((e == 0) & (f == 0))
    def _():
        for q in range(MOE_STEPS):
            issue(0, q, 0)
            consume(q, 0, 0)

    for q in range(MOE_STEPS):
        @pl.when(f == q)
        def _(q=q):
            if q == 0:
                @pl.when(e > 0)
                def _():
                    consume(MOE_STEPS - 1, (MOE_STEPS - 1) % 2, e % 2)
            else:
                @pl.when(nxt < N_EXPERTS)
                def _():
                    consume(q - 1, (q - 1) % 2, nxt % 2)

            @pl.when(nxt < N_EXPERTS)
            def _():
                issue(nxt, q, q % 2)

    x = xb[e % 2]
    a = jnp.dot(x, wg_ref[0].astype(BF16), preferred_element_type=F32)
    u = jnp.dot(x, wu_ref[0].astype(BF16), preferred_element_type=F32)
    o_ref[0] = (a * jax.nn.sigmoid(a) * u).astype(BF16)


def _moe1(idx_c, idx_l, h_c, h_l, w_gate, w_up):
    wspec = pl.BlockSpec((1, D_MODEL, MOE_TF), lambda e, f, ic, il: (e, 0, f))
    grid_spec = pltpu.PrefetchScalarGridSpec(
        num_scalar_prefetch=2,
        grid=(N_EXPERTS, MOE_STEPS),
        in_specs=[pl.BlockSpec(memory_space=pl.ANY), pl.BlockSpec(memory_space=pl.ANY), wspec, wspec],
        out_specs=pl.BlockSpec((1, CAP_ALL, MOE_TF), lambda e, f, ic, il: (e, 0, f)),
        scratch_shapes=[pltpu.VMEM((2, PIECE, D_MODEL), F32),
                        pltpu.VMEM((2, CAP_ALL, D_MODEL), BF16),
                        pltpu.SemaphoreType.DMA((2,))])
    return pl.pallas_call(
        _moe1_body,
        grid_spec=grid_spec,
        out_shape=jax.ShapeDtypeStruct((N_EXPERTS, CAP_ALL, EXPERT_FF), BF16),
        compiler_params=_cparams("arbitrary", "arbitrary", row_dmas=True),
        name="moe1",
    )(idx_c, idx_l, h_c, h_l, w_gate, w_up)


def _moe2_body(h_ref, wd_ref, gc_ref, gl_ref, yc_ref, yl_ref):
    y = jnp.dot(h_ref[0], wd_ref[0].astype(BF16), preferred_element_type=F32)
    yc_ref[0] = y[:CAP_CTX] * gc_ref[0]
    yl_ref[0] = y[CAP_CTX:] * gl_ref[0]


def _moe2(hact, w_down, gate_c, gate_l):
    tn = 512
    return pl.pallas_call(
        _moe2_body,
        grid=(N_EXPERTS, D_MODEL // tn),
        in_specs=[pl.BlockSpec((1, CAP_ALL, EXPERT_FF), lambda e, t: (e, 0, 0)),
                  pl.BlockSpec((1, EXPERT_FF, tn), lambda e, t: (e, 0, t)),
                  pl.BlockSpec((1, CAP_CTX, 1), lambda e, t: (e, 0, 0)),
                  pl.BlockSpec((1, CAP_LAT, 1), lambda e, t: (e, 0, 0))],
        out_specs=[pl.BlockSpec((1, CAP_CTX, tn), lambda e, t: (e, 0, t)),
                   pl.BlockSpec((1, CAP_LAT, tn), lambda e, t: (e, 0, t))],
        out_shape=[jax.ShapeDtypeStruct((N_EXPERTS, CAP_CTX, D_MODEL), F32),
                   jax.ShapeDtypeStruct((N_EXPERTS, CAP_LAT, D_MODEL), F32)],
        compiler_params=_cparams("arbitrary", "arbitrary"),
        name="moe2",
    )(hact, w_down, gate_c[..., None], gate_l[..., None])


def _combine_body(idx_ref, y_ref, out_ref, buf, sem_r, sem_w, *, cap, n):
    e = pl.program_id(0)

    def wait_rows(sem):
        pltpu.make_async_copy(out_ref.at[pl.ds(0, cap), :], buf, sem).wait()

    @pl.when(e == 0)
    def _():
        buf[...] = jnp.zeros_like(buf)
        chunks = n // cap
        for c in range(chunks):
            pltpu.make_async_copy(buf, out_ref.at[pl.ds(c * cap, cap), :], sem_w).start()
        for c in range(chunks):
            pltpu.make_async_copy(buf, out_ref.at[pl.ds(c * cap, cap), :], sem_w).wait()

    def read_row(j, carry):
        pltpu.make_async_copy(out_ref.at[pl.ds(idx_ref[0, 0, j], 1), :], buf.at[pl.ds(j, 1), :], sem_r).start()
        return carry

    lax.fori_loop(0, cap, read_row, 0, unroll=8)
    wait_rows(sem_r)
    buf[...] += y_ref[0]

    def write_row(j, carry):
        pltpu.make_async_copy(buf.at[pl.ds(j, 1), :], out_ref.at[pl.ds(idx_ref[0, 0, j], 1), :], sem_w).start()
        return carry

    lax.fori_loop(0, cap, write_row, 0, unroll=8)
    wait_rows(sem_w)


def _combine(idx, y, n):
    cap = idx.shape[1]
    return pl.pallas_call(
        functools.partial(_combine_body, cap=cap, n=n),
        grid=(N_EXPERTS,),
        in_specs=[pl.BlockSpec((1, 1, cap), lambda e: (e, 0, 0), memory_space=pltpu.SMEM),
                  pl.BlockSpec((1, cap, D_MODEL), lambda e: (e, 0, 0))],
        out_specs=pl.BlockSpec(memory_space=pl.ANY),
        out_shape=jax.ShapeDtypeStruct((n, D_MODEL), F32),
        scratch_shapes=[pltpu.VMEM((cap, D_MODEL), F32), pltpu.SemaphoreType.DMA, pltpu.SemaphoreType.DMA],
        compiler_params=_cparams("arbitrary", row_dmas=True),
        name=f"combine_{cap}",
    )(idx.reshape(N_EXPERTS, 1, cap), y)


def _final_body(x1_ref, moe_ref, mod_ref, fn_ref, o_ref):
    x = x1_ref[...] + mod_ref[0, 5:6, :] * moe_ref[...]
    ms = jnp.mean(x * x, axis=-1, keepdims=True)
    o_ref[...] = x * lax.rsqrt(ms + EPS) * fn_ref[...]


def _final(x1, moe, mod, final_norm, tokens_per_mod):
    n = x1.shape[0]
    tm = 512
    row = pl.BlockSpec((tm, D_MODEL), lambda i: (i, 0))
    return pl.pallas_call(
        _final_body,
        grid=(n // tm,),
        in_specs=[row, row,
                  pl.BlockSpec((1, N_MOD, D_MODEL), lambda i: (i * tm // tokens_per_mod, 0, 0)),
                  pl.BlockSpec((1, D_MODEL), lambda i: (0, 0))],
        out_specs=row,
        out_shape=jax.ShapeDtypeStruct((n, D_MODEL), F32),
        compiler_params=_cparams("arbitrary"),
        name="final",
    )(x1, moe, mod, final_norm)


def _mixer(x, mod, tokens_per_mod, seq, w, rope_tabs, cache=None):
    if rope_tabs is None:
        z, kv = _inproj(x, mod, w["norm_mix"], w["w_in"], tokens_per_mod)
        attn = _attn_ctx(z, w["sink"])
    else:
        z = _inproj(x, mod, w["norm_mix"], w["w_in"], tokens_per_mod, rope_tabs)
        kv = None
        attn = _attn_lat(z, cache[0], cache[1], w["sink"])
    fmix = _fourier(z, seq)
    mrg = _merge1(fmix, attn, z, w["w_fo"], w["w_ao"])
    x1, hp, aff_t = _merge2(mrg, x, mod, w["norm_ffn"], w["w_out"], w["w_rt"], tokens_per_mod)
    return kv, x1, hp, aff_t


def kernel(x_prompt, x_sample, cache_k, cache_v, c, c_ctx, w_mod, b_mod, norm_mix, w_in, attn_sink,
           w_fourier_out, w_attn_out, w_merge_out, norm_ffn, w_router, w_exp_gate, w_exp_up,
           w_exp_down, final_norm):
    l = 0
    cv = jnp.zeros((16, D_MODEL), F32).at[0].set(c_ctx).at[1:1 + DEC_BATCH].set(c)
    mod = _ada(cv, w_mod[l], b_mod[l])[:1 + DEC_BATCH].reshape(1 + DEC_BATCH, N_MOD, D_MODEL)
    mod_ctx, mod_lat = mod[:1], mod[1:]

    w = {
        "norm_mix": norm_mix[l].reshape(1, D_MODEL),
        "norm_ffn": norm_ffn[l].reshape(1, D_MODEL),
        "w_in": w_in[l].astype(BF16),
        "w_fo": w_fourier_out[l].astype(BF16),
        "w_ao": w_attn_out[l].astype(BF16),
        "w_out": w_merge_out[l].astype(BF16),
        "w_rt": w_router[l].T.astype(BF16),
        "sink": attn_sink[l].reshape(1, N_HEADS),
    }
    xp = x_prompt.reshape(N_CTX, D_MODEL)
    xs = x_sample.reshape(N_LAT, D_MODEL)
    ck = cache_k[:, l].reshape(DEC_BATCH, PAST_LEN, KV_DIM)
    cvv = cache_v[:, l].reshape(DEC_BATCH, PAST_LEN, KV_DIM)

    kv, x1_c, hp_c, aff_c = _mixer(xp, mod_ctx, N_CTX, SEQ, w, None)
    _, x1_l, hp_l, aff_l = _mixer(xs, mod_lat, DEC_SEQ, DEC_SEQ, w, _rope_tables(), (ck, cvv))

    idx_c, gate_c = _route(aff_c, CAP_CTX)
    idx_l, gate_l = _route(aff_l, CAP_LAT)
    hact = _moe1(idx_c, idx_l, hp_c, hp_l, w_exp_gate[l], w_exp_up[l])
    y_c, y_l = _moe2(hact, w_exp_down[l], gate_c, gate_l)
    moe_c = _combine(idx_c, y_c, N_CTX)
    moe_l = _combine(idx_l, y_l, N_LAT)

    fn = final_norm.reshape(1, D_MODEL)
    y_prompt = _final(x1_c, moe_c, mod_ctx, fn, N_CTX).reshape(BATCH, SEQ, D_MODEL)
    y_sample = _final(x1_l, moe_l, mod_lat, fn, DEC_SEQ).reshape(DEC_BATCH, DEC_SEQ, D_MODEL)
    state_k = kv[:, :KV_DIM].reshape(BATCH, 1, SEQ, N_KV_HEADS, HEAD_DIM)
    state_v = kv[:, KV_DIM:].reshape(BATCH, 1, SEQ, N_KV_HEADS, HEAD_DIM)
    return (y_prompt, y_sample, state_k, state_v)
```

```python
import functools
import math

import numpy as np
import jax
import jax.numpy as jnp
from jax import lax
from jax.experimental import pallas as pl
from jax.experimental.pallas import tpu as pltpu

D_MODEL = 2048
BATCH = 16
SEQ = 256
DEC_BATCH = 4
DEC_SEQ = 2048
PAST_LEN = 256
GRID_W = 64
N_HEADS = 16
N_KV_HEADS = 4
HEAD_DIM = 128
Q_GROUPS = N_HEADS // N_KV_HEADS
WINDOW = 128
ROPE_THETA = 10000.0
FOURIER_GROUPS = 4
FOURIER_GROUP_DIM = 256
FOURIER_DIM = FOURIER_GROUPS * FOURIER_GROUP_DIM
N_EXPERTS = 16
EXPERT_FF = 2048
CAPACITY_FACTOR = 2
Q_DIM = N_HEADS * HEAD_DIM
KV_DIM = N_KV_HEADS * HEAD_DIM
IN_DIM = FOURIER_DIM + Q_DIM + 2 * KV_DIM + 2 * D_MODEL
N_MOD = 6
EPS = 1e-6
NEG_INF = -1e30

F32 = jnp.float32
BF16 = jnp.bfloat16
LANES = 128
VMEM_LIMIT = 56 * 1024 * 1024
N_CTX = BATCH * SEQ
N_LAT = DEC_BATCH * DEC_SEQ
CAP_CTX = CAPACITY_FACTOR * N_CTX // N_EXPERTS
CAP_LAT = CAPACITY_FACTOR * N_LAT // N_EXPERTS
CAP_ALL = CAP_CTX + CAP_LAT


def _cparams(*sem, row_dmas=False):
    return pltpu.CompilerParams(dimension_semantics=sem, vmem_limit_bytes=VMEM_LIMIT,
                                disable_bounds_checks=row_dmas)


def _split_bf16(x):
    hi = x.astype(BF16)
    lo = (x - hi.astype(F32)).astype(BF16)
    return hi, lo


def _ada_body(cv_ref, w_ref, b_ref, o_ref):
    cv = cv_ref[...]
    s = cv * jax.nn.sigmoid(cv)
    s_hi, s_lo = _split_bf16(s)
    w_hi, w_lo = _split_bf16(w_ref[...])
    acc = jnp.dot(s_hi, w_hi, preferred_element_type=F32)
    acc += jnp.dot(s_hi, w_lo, preferred_element_type=F32)
    acc += jnp.dot(s_lo, w_hi, preferred_element_type=F32)
    o_ref[...] = acc + b_ref[...]


def _ada(cv, w_mod, b_mod):
    rows = cv.shape[0]
    tn = 1024
    return pl.pallas_call(
        _ada_body,
        grid=(N_MOD * D_MODEL // tn,),
        in_specs=[pl.BlockSpec((rows, D_MODEL), lambda j: (0, 0)),
                  pl.BlockSpec((D_MODEL, tn), lambda j: (0, j)),
                  pl.BlockSpec((1, tn), lambda j: (0, j))],
        out_specs=pl.BlockSpec((rows, tn), lambda j: (0, j)),
        out_shape=jax.ShapeDtypeStruct((rows, N_MOD * D_MODEL), F32),
        compiler_params=_cparams("arbitrary"),
        name="ada",
    )(cv, w_mod, b_mod.reshape(1, -1))


def _rmsnorm_mod(x, gain, shift, scale):
    ms = jnp.mean(x * x, axis=-1, keepdims=True)
    y = x * lax.rsqrt(ms + EPS) * gain
    return y * (1.0 + scale) + shift


def _rope_chunk(c, cos, sin_signed, low):
    partner = jnp.where(low, pltpu.roll(c, 96, 1), pltpu.roll(c, 32, 1))
    return c * cos + partner * sin_signed


def _inproj_body(*refs, rope, tn):
    if rope:
        x_ref, mod_ref, gn_ref, w_ref, cos_ref, sin_ref, z_ref, h_scr = refs
    else:
        x_ref, mod_ref, gn_ref, w_ref, z_ref, kv_ref, h_scr = refs
    j = pl.program_id(1)

    @pl.when(j == 0)
    def _():
        h = _rmsnorm_mod(x_ref[...], gn_ref[...], mod_ref[0, 0:1, :], mod_ref[0, 1:2, :])
        h_scr[...] = h.astype(BF16)

    acc = jnp.dot(h_scr[...], w_ref[...], preferred_element_type=F32)
    heads_per_tile = tn // HEAD_DIM
    q_lo, q_hi = FOURIER_DIM // tn, (FOURIER_DIM + Q_DIM) // tn

    if not rope:
        z_ref[...] = acc.astype(BF16)

        @pl.when(j == q_hi)
        def _():
            kv_ref[...] = acc
    else:
        def roped(nheads):
            cos = cos_ref[...]
            sin = sin_ref[...]
            low = (lax.broadcasted_iota(jnp.int32, cos.shape, 1) & 32) == 0
            for hd in range(nheads):
                sl = slice(hd * HEAD_DIM, (hd + 1) * HEAD_DIM)
                z_ref[:, sl] = _rope_chunk(acc[:, sl], cos, sin, low).astype(BF16)

        @pl.when((j < q_lo) | (j > q_hi))
        def _():
            z_ref[...] = acc.astype(BF16)

        @pl.when((j >= q_lo) & (j < q_hi))
        def _():
            roped(heads_per_tile)

        @pl.when(j == q_hi)
        def _():
            roped(N_KV_HEADS)
            z_ref[:, KV_DIM:] = acc[:, KV_DIM:].astype(BF16)


def _inproj(x, mod, gain, w_in_b, tokens_per_mod, rope_tabs=None):
    n = x.shape[0]
    tm, tn = 1024, 1024
    rope = rope_tabs is not None
    in_specs = [pl.BlockSpec((tm, D_MODEL), lambda i, j: (i, 0)),
                pl.BlockSpec((1, N_MOD, D_MODEL), lambda i, j: (i * tm // tokens_per_mod, 0, 0)),
                pl.BlockSpec((1, D_MODEL), lambda i, j: (0, 0)),
                pl.BlockSpec((D_MODEL, tn), lambda i, j: (0, j))]
    args = [x, mod, gain, w_in_b]
    z_spec = pl.BlockSpec((tm, tn), lambda i, j: (i, j))
    z_shape = jax.ShapeDtypeStruct((n, IN_DIM), BF16)
    if rope:
        tiles_per_seq = DEC_SEQ // tm
        in_specs += [pl.BlockSpec((tm, HEAD_DIM), lambda i, j: (i % tiles_per_seq, 0))] * 2
        args += list(rope_tabs)
        out_specs, out_shape = z_spec, z_shape
    else:
        out_specs = [z_spec, pl.BlockSpec((tm, 2 * KV_DIM), lambda i, j: (i, 0))]
        out_shape = [z_shape, jax.ShapeDtypeStruct((n, 2 * KV_DIM), F32)]
    return pl.pallas_call(
        functools.partial(_inproj_body, rope=rope, tn=tn),
        grid=(n // tm, IN_DIM // tn),
        in_specs=in_specs, out_specs=out_specs, out_shape=out_shape,
        scratch_shapes=[pltpu.VMEM((tm, D_MODEL), BF16)],
        compiler_params=_cparams("arbitrary", "arbitrary"),
        name="inproj_lat" if rope else "inproj_ctx",
    )(*args)


def _rope_tables():
    d = np.arange(HEAD_DIM)
    half, dd = d // 64, d % 64
    inv = ROPE_THETA ** (-jnp.arange(0, 64, 2, dtype=F32) / 64)
    inv_d = inv[jnp.asarray(dd % 32)]
    s = np.arange(DEC_SEQ)
    pos = np.where(half[None, :] == 0, (s // GRID_W)[:, None], (s % GRID_W)[:, None])
    ang = jnp.asarray(pos, F32) * inv_d[None, :]
    sign = jnp.asarray(np.where(dd < 32, -1.0, 1.0), F32)
    return jnp.cos(ang), jnp.sin(ang) * sign[None, :]


def _fourier_body(u_ref, cc_ref, cs_ref, o_ref, pq_scr, *, seq):
    @pl.when(pl.program_id(1) == 0)
    def _():
        for g in range(FOURIER_GROUPS):
            sl = slice(g * FOURIER_GROUP_DIM, (g + 1) * FOURIER_GROUP_DIM)
            pq = jnp.dot(u_ref[:, sl], cc_ref[...], preferred_element_type=F32)
            pq_scr[0:seq, sl] = pq[:, :FOURIER_GROUP_DIM].astype(BF16)
            pq_scr[seq:2 * seq, sl] = pq[:, FOURIER_GROUP_DIM:].astype(BF16)

    o_ref[...] = jnp.dot(cs_ref[...], pq_scr[...], preferred_element_type=F32).astype(BF16)


def _dft_mats(n):
    k = np.arange(n)
    ang = 2.0 * np.pi * ((k[:, None] * k[None, :]) % n) / n
    return np.cos(ang) / math.sqrt(n), np.sin(ang) / math.sqrt(n)


def _fourier(z, seq):
    n = z.shape[0]
    tr = min(seq, 512)
    cc, sc = _dft_mats(FOURIER_GROUP_DIM)
    cs, ss = _dft_mats(seq)
    chan = jnp.asarray(np.concatenate([cc, sc], axis=1), F32).astype(BF16)
    posm = jnp.asarray(np.concatenate([cs, -ss], axis=1), F32).astype(BF16)
    tiles = seq // tr
    return pl.pallas_call(
        functools.partial(_fourier_body, seq=seq),
        grid=(n // seq, tiles),
        in_specs=[pl.BlockSpec((seq, FOURIER_DIM), lambda b, i: (b, 0)),
                  pl.BlockSpec((FOURIER_GROUP_DIM, 2 * FOURIER_GROUP_DIM), lambda b, i: (0, 0)),
                  pl.BlockSpec((tr, 2 * seq), lambda b, i: (i, 0))],
        out_specs=pl.BlockSpec((tr, FOURIER_DIM), lambda b, i: (b * tiles + i, 0)),
        out_shape=jax.ShapeDtypeStruct((n, FOURIER_DIM), BF16),
        scratch_shapes=[pltpu.VMEM((2 * seq, FOURIER_DIM), BF16)],
        compiler_params=_cparams("arbitrary", "arbitrary"),
        name=f"fourier_{seq}",
    )(z, chan, posm)


_NT = (((1,), (1,)), ((), ()))
_SCALE = HEAD_DIM ** -0.5


def _attn_ctx_body(sink_ref, q_ref, k_ref, v_ref, o_ref):
    h = pl.program_id(1)
    k = k_ref[...]
    v = v_ref[...]
    for g in range(Q_GROUPS):
        sl = slice(g * HEAD_DIM, (g + 1) * HEAD_DIM)
        sk = sink_ref[0, h * Q_GROUPS + g]
        s = lax.dot_general(q_ref[:, sl], k, _NT, preferred_element_type=F32) * _SCALE
        m = jnp.maximum(jnp.max(s, axis=-1, keepdims=True), sk)
        p = jnp.exp(s - m)
        denom = jnp.sum(p, axis=-1, keepdims=True) + jnp.exp(sk - m)
        o = jnp.dot(p.astype(BF16), v, preferred_element_type=F32) / denom
        o_ref[:, sl] = o.astype(BF16)


def _attn_ctx(z, sink):
    n = z.shape[0]
    gw = Q_GROUPS * HEAD_DIM
    q0 = FOURIER_DIM // gw
    k0 = (FOURIER_DIM + Q_DIM) // HEAD_DIM
    v0 = k0 + N_KV_HEADS
    return pl.pallas_call(
        _attn_ctx_body,
        grid=(n // SEQ, N_KV_HEADS),
        in_specs=[pl.BlockSpec(memory_space=pltpu.SMEM),
                  pl.BlockSpec((SEQ, gw), lambda b, h: (b, q0 + h)),
                  pl.BlockSpec((SEQ, HEAD_DIM), lambda b, h: (b, k0 + h)),
                  pl.BlockSpec((SEQ, HEAD_DIM), lambda b, h: (b, v0 + h))],
        out_specs=pl.BlockSpec((SEQ, gw), lambda b, h: (b, h)),
        out_shape=jax.ShapeDtypeStruct((n, Q_DIM), BF16),
        compiler_params=_cparams("arbitrary", "arbitrary"),
        name="attn_ctx",
    )(sink, z, z, z)


ATT_TQ = 256
ATT_WIN = ATT_TQ + 2 * WINDOW


def _attn_lat_body(sink_ref, q_ref, k_ref, v_ref, ck_ref, cv_ref, o_ref):
    h = pl.program_id(1)
    nq = pl.program_id(2)
    start = jnp.clip(nq * ATT_TQ - WINDOW, 0, DEC_SEQ - ATT_WIN)
    start = pl.multiple_of(start, WINDOW)
    kl = k_ref[pl.ds(start, ATT_WIN), :]
    vl = v_ref[pl.ds(start, ATT_WIN), :]
    kc = ck_ref[0].astype(BF16)
    vc = cv_ref[0].astype(BF16)
    qpos = nq * ATT_TQ + lax.broadcasted_iota(jnp.int32, (ATT_TQ, ATT_WIN), 0)
    kpos = start + lax.broadcasted_iota(jnp.int32, (ATT_TQ, ATT_WIN), 1)
    valid = jnp.abs(qpos - kpos) <= WINDOW
    for g in range(Q_GROUPS):
        sl = slice(g * HEAD_DIM, (g + 1) * HEAD_DIM)
        sk = sink_ref[0, h * Q_GROUPS + g]
        q = q_ref[:, sl]
        s_loc = lax.dot_general(q, kl, _NT, preferred_element_type=F32) * _SCALE
        s_loc = jnp.where(valid, s_loc, NEG_INF)
        s_ctx = lax.dot_general(q, kc, _NT, preferred_element_type=F32) * _SCALE
        m = jnp.maximum(jnp.maximum(jnp.max(s_loc, axis=-1, keepdims=True),
                                    jnp.max(s_ctx, axis=-1, keepdims=True)), sk)
        p_loc = jnp.exp(s_loc - m)
        p_ctx = jnp.exp(s_ctx - m)
        denom = (jnp.sum(p_loc, axis=-1, keepdims=True) + jnp.sum(p_ctx, axis=-1, keepdims=True)
                 + jnp.exp(sk - m))
        o = (jnp.dot(p_loc.astype(BF16), vl, preferred_element_type=F32)
             + jnp.dot(p_ctx.astype(BF16), vc, preferred_element_type=F32)) / denom
        o_ref[:, sl] = o.astype(BF16)


def _attn_lat(z, cache_k, cache_v, sink):
    n = z.shape[0]
    gw = Q_GROUPS * HEAD_DIM
    q0 = FOURIER_DIM // gw
    k0 = (FOURIER_DIM + Q_DIM) // HEAD_DIM
    v0 = k0 + N_KV_HEADS
    tiles = DEC_SEQ // ATT_TQ
    return pl.pallas_call(
        _attn_lat_body,
        grid=(n // DEC_SEQ, N_KV_HEADS, tiles),
        in_specs=[pl.BlockSpec(memory_space=pltpu.SMEM),
                  pl.BlockSpec((ATT_TQ, gw), lambda b, h, t: (b * tiles + t, q0 + h)),
                  pl.BlockSpec((DEC_SEQ, HEAD_DIM), lambda b, h, t: (b, k0 + h)),
                  pl.BlockSpec((DEC_SEQ, HEAD_DIM), lambda b, h, t: (b, v0 + h)),
                  pl.BlockSpec((1, PAST_LEN, HEAD_DIM), lambda b, h, t: (b, 0, h)),
                  pl.BlockSpec((1, PAST_LEN, HEAD_DIM), lambda b, h, t: (b, 0, h))],
        out_specs=pl.BlockSpec((ATT_TQ, gw), lambda b, h, t: (b * tiles + t, h)),
        out_shape=jax.ShapeDtypeStruct((n, Q_DIM), BF16),
        compiler_params=_cparams("arbitrary", "arbitrary", "arbitrary"),
        name="attn_lat",
    )(sink, z, z, z, cache_k, cache_v)


def _merge1_body(f_ref, a_ref, gf_ref, ga_ref, wf_ref, wa_ref, o_ref):
    f = jnp.dot(f_ref[...], wf_ref[...], preferred_element_type=F32)
    a = jnp.dot(a_ref[...], wa_ref[...], preferred_element_type=F32)
    gf = jax.nn.sigmoid(gf_ref[...].astype(F32))
    ga = jax.nn.sigmoid(ga_ref[...].astype(F32))
    o_ref[...] = (gf * f + ga * a).astype(BF16)


def _merge1(fmix, attn, z, w_fo_b, w_ao_b):
    n = fmix.shape[0]
    tm = 512
    g0 = (FOURIER_DIM + Q_DIM + 2 * KV_DIM) // D_MODEL
    return pl.pallas_call(
        _merge1_body,
        grid=(n // tm,),
        in_specs=[pl.BlockSpec((tm, FOURIER_DIM), lambda i: (i, 0)),
                  pl.BlockSpec((tm, Q_DIM), lambda i: (i, 0)),
                  pl.BlockSpec((tm, D_MODEL), lambda i: (i, g0)),
                  pl.BlockSpec((tm, D_MODEL), lambda i: (i, g0 + 1)),
                  pl.BlockSpec((FOURIER_DIM, D_MODEL), lambda i: (0, 0)),
                  pl.BlockSpec((Q_DIM, D_MODEL), lambda i: (0, 0))],
        out_specs=pl.BlockSpec((tm, D_MODEL), lambda i: (i, 0)),
        out_shape=jax.ShapeDtypeStruct((n, D_MODEL), BF16),
        compiler_params=_cparams("arbitrary"),
        name="merge1",
    )(fmix, attn, z, z, w_fo_b, w_ao_b)


def _merge2_body(m_ref, x_ref, mod_ref, gn_ref, wo_ref, wr_ref, x1_ref, hp_ref, aff_ref):
    out = jnp.dot(m_ref[...], wo_ref[...], preferred_element_type=F32)
    x1 = x_ref[...] + mod_ref[0, 2:3, :] * out
    x1_ref[...] = x1
    h2 = _rmsnorm_mod(x1, gn_ref[...], mod_ref[0, 3:4, :], mod_ref[0, 4:5, :])
    hb = h2.astype(BF16)
    hp_ref[...] = hb.astype(F32)
    logits = lax.dot_general(wr_ref[...], hb, _NT, preferred_element_type=F32)
    mx = jnp.max(logits, axis=0, keepdims=True)
    ex = jnp.exp(logits - mx)
    aff_ref[...] = ex / jnp.sum(ex, axis=0, keepdims=True)


def _merge2(mrg, x, mod, gain, w_out_b, w_rt_b, tokens_per_mod):
    n = x.shape[0]
    tm = 512
    return pl.pallas_call(
        _merge2_body,
        grid=(n // tm,),
        in_specs=[pl.BlockSpec((tm, D_MODEL), lambda i: (i, 0)),
                  pl.BlockSpec((tm, D_MODEL), lambda i: (i, 0)),
                  pl.BlockSpec((1, N_MOD, D_MODEL), lambda i: (i * tm // tokens_per_mod, 0, 0)),
                  pl.BlockSpec((1, D_MODEL), lambda i: (0, 0)),
                  pl.BlockSpec((D_MODEL, D_MODEL), lambda i: (0, 0)),
                  pl.BlockSpec((N_EXPERTS, D_MODEL), lambda i: (0, 0))],
        out_specs=[pl.BlockSpec((tm, D_MODEL), lambda i: (i, 0)),
                   pl.BlockSpec((tm, D_MODEL), lambda i: (i, 0)),
                   pl.BlockSpec((N_EXPERTS, tm), lambda i: (0, i))],
        out_shape=[jax.ShapeDtypeStruct((n, D_MODEL), F32),
                   jax.ShapeDtypeStruct((n, D_MODEL), F32),
                   jax.ShapeDtypeStruct((N_EXPERTS, n), F32)],
        compiler_params=_cparams("arbitrary"),
        name="merge2",
    )(mrg, x, mod, gain, w_out_b, w_rt_b)


ROUTE_JB = 256
COL_TOK = 0
COL_GATE = 8


def _cumsum_lanes(dst_ref, mask, n):
    x = jnp.where(mask, 1.0, 0.0).astype(BF16)
    tri = (lax.broadcasted_iota(jnp.int32, (LANES, LANES), 0)
           <= lax.broadcasted_iota(jnp.int32, (LANES, LANES), 1)).astype(BF16)
    off = jnp.zeros((N_EXPERTS, 1), F32)
    for c in range(n // LANES):
        sl = slice(c * LANES, (c + 1) * LANES)
        pc = jnp.dot(x[:, sl], tri, preferred_element_type=F32) + off
        dst_ref[:, sl] = pc
        off = pc[:, LANES - 1:LANES]


def _route_body(aff_ref, res_ref, key_scr, m_scr, r_scr, *, n, cap):
    e = pl.program_id(0)

    @pl.when(e == 0)
    def _():
        a = aff_ref[...]
        def search(k, thr):
            cand = thr | lax.shift_left(jnp.int32(1), 30 - k)
            cnt = jnp.sum(jnp.where(a >= lax.bitcast_convert_type(cand, F32), 1.0, 0.0), axis=1, keepdims=True)
            return jnp.where(cnt >= cap, cand, thr)

        thr_bits = lax.fori_loop(0, 31, search, jnp.zeros((N_EXPERTS, 1), jnp.int32))
        thr = lax.bitcast_convert_type(thr_bits, F32)
        gt = a > thr
        eq = a == thr
        need = cap - jnp.sum(jnp.where(gt, 1.0, 0.0), axis=1, keepdims=True)
        _cumsum_lanes(key_scr, eq, n)
        sel = gt | (eq & (key_scr[...] <= need))
        _cumsum_lanes(key_scr, sel, n)
        key_scr[...] = jnp.where(sel, key_scr[...] - 1.0, -1.0)

        tok = lax.broadcasted_iota(jnp.int32, (8, n), 1)
        row = lax.broadcasted_iota(jnp.int32, (8, n), 0)
        m_scr[0:8, :] = jnp.where(row == 0, tok // LANES, jnp.where(row == 1, tok % LANES, 0)).astype(F32)
        a_hi = a.astype(BF16).astype(F32)
        a_mid = (a - a_hi).astype(BF16).astype(F32)
        a_lo = (a - a_hi - a_mid).astype(BF16).astype(F32)
        m_scr[COL_GATE:COL_GATE + N_EXPERTS, :] = a_hi
        m_scr[COL_GATE + N_EXPERTS:COL_GATE + 2 * N_EXPERTS, :] = a_mid
        m_scr[COL_GATE + 2 * N_EXPERTS:COL_GATE + 3 * N_EXPERTS, :] = a_lo
        m_scr[COL_GATE + 3 * N_EXPERTS:, :] = jnp.zeros((LANES - COL_GATE - 3 * N_EXPERTS, n), F32)
        for c in range(n // LANES):
            sl = slice(c * LANES, (c + 1) * LANES)
            r_scr[sl, :] = m_scr[:, sl].T.astype(BF16)

    key = key_scr[pl.ds(e, 1), :]
    for jb in range(cap // ROUTE_JB):
        slot = (lax.broadcasted_iota(jnp.int32, (ROUTE_JB, 1), 0) + jb * ROUTE_JB).astype(F32)
        onehot = jnp.where(key == slot, 1.0, 0.0).astype(BF16)
        res_ref[0, jb * ROUTE_JB:(jb + 1) * ROUTE_JB, :] = jnp.dot(onehot, r_scr[...], preferred_element_type=F32)


def _route(aff_t, cap):
    n = aff_t.shape[1]
    res = pl.pallas_call(
        functools.partial(_route_body, n=n, cap=cap),
        grid=(N_EXPERTS,),
        in_specs=[pl.BlockSpec((N_EXPERTS, n), lambda e: (0, 0))],
        out_specs=pl.BlockSpec((1, cap, LANES), lambda e: (e, 0, 0)),
        out_shape=jax.ShapeDtypeStruct((N_EXPERTS, cap, LANES), F32),
        scratch_shapes=[pltpu.VMEM((N_EXPERTS, n), F32),
                        pltpu.VMEM((LANES, n), F32),
                        pltpu.VMEM((n, LANES), BF16)],
        compiler_params=_cparams("arbitrary"),
        name=f"route_{n}",
    )(aff_t)
    idx = (res[:, :, COL_TOK] * LANES + res[:, :, COL_TOK + 1]).astype(jnp.int32)
    ar = jnp.arange(N_EXPERTS)
    gate = (res[ar, :, COL_GATE + ar] + res[ar, :, COL_GATE + N_EXPERTS + ar]
            + res[ar, :, COL_GATE + 2 * N_EXPERTS + ar])
    return idx, gate


MOE_TF = 512
MOE_STEPS = EXPERT_FF // MOE_TF
PIECE = CAP_ALL // MOE_STEPS


def _piece_parts(q):
    lo, hi = q * PIECE, (q + 1) * PIECE
    parts = []
    if lo < CAP_CTX:
        parts.append((0, lo, min(hi, CAP_CTX)))
    if hi > CAP_CTX:
        parts.append((1, max(lo, CAP_CTX) - CAP_CTX, hi - CAP_CTX))
    return parts


def _moe1_body(idxc_ref, idxl_ref, hc_ref, hl_ref, wg_ref, wu_ref, o_ref, stage, xb, sems):
    e = pl.program_id(0)
    f = pl.program_id(1)
    nxt = e + 1
    idx_refs, h_refs = (idxc_ref, idxl_ref), (hc_ref, hl_ref)

    def issue(expert, q, slot):
        for grp, first, last in _piece_parts(q):
            off = (CAP_CTX if grp else 0) - q * PIECE

            half = (last - first) // 2

            def body(j, carry, grp=grp, off=off, half=half):
                for queue in (0, 1):
                    jj = j + queue * half
                    t = idx_refs[grp][expert, jj]
                    pltpu.make_async_copy(h_refs[grp].at[pl.ds(t, 1), :], stage.at[slot, pl.ds(jj + off, 1), :],
                                          sems.at[slot]).start(priority=queue)
                return carry

            lax.fori_loop(first, first + half, body, 0, unroll=4)

    def consume(q, slot, dst):
        pltpu.make_async_copy(hl_ref.at[pl.ds(0, PIECE), :], stage.at[slot], sems.at[slot]).wait()
        xb[dst, q * PIECE:(q + 1) * PIECE, :] = stage[slot].astype(BF16)

    @pl.when((e == 0) & (f == 0))
    def _():
        for q in range(MOE_STEPS):
            issue(0, q, 0)
            consume(q, 0, 0)

    for q in range(MOE_STEPS):
        @pl.when(f == q)
        def _(q=q):
            if q == 0:
                @pl.when(e > 0)
                def _():
                    consume(MOE_STEPS - 1, (MOE_STEPS - 1) % 2, e % 2)
            else:
                @pl.when(nxt < N_EXPERTS)
                def _():
                    consume(q - 1, (q - 1) % 2, nxt % 2)

            @pl.when(nxt < N_EXPERTS)
            def _():
                issue(nxt, q, q % 2)

    x = xb[e % 2]
    a = jnp.dot(x, wg_ref[0].astype(BF16), preferred_element_type=F32)
    u = jnp.dot(x, wu_ref[0].astype(BF16), preferred_element_type=F32)
    o_ref[0] = (a * jax.nn.sigmoid(a) * u).astype(BF16)


def _moe1(idx_c, idx_l, h_c, h_l, w_gate, w_up):
    wspec = pl.BlockSpec((1, D_MODEL, MOE_TF), lambda e, f, ic, il: (e, 0, f))
    grid_spec = pltpu.PrefetchScalarGridSpec(
        num_scalar_prefetch=2,
        grid=(N_EXPERTS, MOE_STEPS),
        in_specs=[pl.BlockSpec(memory_space=pl.ANY), pl.BlockSpec(memory_space=pl.ANY), wspec, wspec],
        out_specs=pl.BlockSpec((1, CAP_ALL, MOE_TF), lambda e, f, ic, il: (e, 0, f)),
        scratch_shapes=[pltpu.VMEM((2, PIECE, D_MODEL), F32),
                        pltpu.VMEM((2, CAP_ALL, D_MODEL), BF16),
                        pltpu.SemaphoreType.DMA((2,))])
    return pl.pallas_call(
        _moe1_body,
        grid_spec=grid_spec,
        out_shape=jax.ShapeDtypeStruct((N_EXPERTS, CAP_ALL, EXPERT_FF), BF16),
        compiler_params=_cparams("arbitrary", "arbitrary", row_dmas=True),
        name="moe1",
    )(idx_c, idx_l, h_c, h_l, w_gate, w_up)


def _moe2_body(h_ref, wd_ref, gc_ref, gl_ref, yc_ref, yl_ref):
    y = jnp.dot(h_ref[0], wd_ref[0].astype(BF16), preferred_element_type=F32)
    yc_ref[0] = y[:CAP_CTX] * gc_ref[0]
    yl_ref[0] = y[CAP_CTX:] * gl_ref[0]


def _moe2(hact, w_down, gate_c, gate_l):
    tn = 512
    return pl.pallas_call(
        _moe2_body,
        grid=(N_EXPERTS, D_MODEL // tn),
        in_specs=[pl.BlockSpec((1, CAP_ALL, EXPERT_FF), lambda e, t: (e, 0, 0)),
                  pl.BlockSpec((1, EXPERT_FF, tn), lambda e, t: (e, 0, t)),
                  pl.BlockSpec((1, CAP_CTX, 1), lambda e, t: (e, 0, 0)),
                  pl.BlockSpec((1, CAP_LAT, 1), lambda e, t: (e, 0, 0))],
        out_specs=[pl.BlockSpec((1, CAP_CTX, tn), lambda e, t: (e, 0, t)),
                   pl.BlockSpec((1, CAP_LAT, tn), lambda e, t: (e, 0, t))],
        out_shape=[jax.ShapeDtypeStruct((N_EXPERTS, CAP_CTX, D_MODEL), F32),
                   jax.ShapeDtypeStruct((N_EXPERTS, CAP_LAT, D_MODEL), F32)],
        compiler_params=_cparams("arbitrary", "arbitrary"),
        name="moe2",
    )(hact, w_down, gate_c[..., None], gate_l[..., None])


def _combine_body(idx_ref, y_ref, out_ref, buf, sem_r, sem_w, *, cap, n):
    e = pl.program_id(0)

    def wait_rows(sem):
        pltpu.make_async_copy(out_ref.at[pl.ds(0, cap), :], buf, sem).wait()

    @pl.when(e == 0)
    def _():
        buf[...] = jnp.zeros_like(buf)
        chunks = n // cap
        for c in range(chunks):
            pltpu.make_async_copy(buf, out_ref.at[pl.ds(c * cap, cap), :], sem_w).start()
        for c in range(chunks):
            pltpu.make_async_copy(buf, out_ref.at[pl.ds(c * cap, cap), :], sem_w).wait()

    half = cap // 2

    def read_row(j, carry):
        for queue in (0, 1):
            jj = j + queue * half
            pltpu.make_async_copy(out_ref.at[pl.ds(idx_ref[0, 0, jj], 1), :], buf.at[pl.ds(jj, 1), :],
                                  sem_r).start(priority=queue)
        return carry

    lax.fori_loop(0, half, read_row, 0, unroll=4)
    wait_rows(sem_r)
    buf[...] += y_ref[0]

    def write_row(j, carry):
        for queue in (0, 1):
            jj = j + queue * half
            pltpu.make_async_copy(buf.at[pl.ds(jj, 1), :], out_ref.at[pl.ds(idx_ref[0, 0, jj], 1), :],
                                  sem_w).start(priority=queue)
        return carry

    lax.fori_loop(0, half, write_row, 0, unroll=4)
    wait_rows(sem_w)


def _combine(idx, y, n):
    cap = idx.shape[1]
    return pl.pallas_call(
        functools.partial(_combine_body, cap=cap, n=n),
        grid=(N_EXPERTS,),
        in_specs=[pl.BlockSpec((1, 1, cap), lambda e: (e, 0, 0), memory_space=pltpu.SMEM),
                  pl.BlockSpec((1, cap, D_MODEL), lambda e: (e, 0, 0))],
        out_specs=pl.BlockSpec(memory_space=pl.ANY),
        out_shape=jax.ShapeDtypeStruct((n, D_MODEL), F32),
        scratch_shapes=[pltpu.VMEM((cap, D_MODEL), F32), pltpu.SemaphoreType.DMA, pltpu.SemaphoreType.DMA],
        compiler_params=_cparams("arbitrary", row_dmas=True),
        name=f"combine_{cap}",
    )(idx.reshape(N_EXPERTS, 1, cap), y)


def _final_body(x1_ref, moe_ref, mod_ref, fn_ref, o_ref):
    x = x1_ref[...] + mod_ref[0, 5:6, :] * moe_ref[...]
    ms = jnp.mean(x * x, axis=-1, keepdims=True)
    o_ref[...] = x * lax.rsqrt(ms + EPS) * fn_ref[...]


def _final(x1, moe, mod, final_norm, tokens_per_mod):
    n = x1.shape[0]
    tm = 512
    row = pl.BlockSpec((tm, D_MODEL), lambda i: (i, 0))
    return pl.pallas_call(
        _final_body,
        grid=(n // tm,),
        in_specs=[row, row,
                  pl.BlockSpec((1, N_MOD, D_MODEL), lambda i: (i * tm // tokens_per_mod, 0, 0)),
                  pl.BlockSpec((1, D_MODEL), lambda i: (0, 0))],
        out_specs=row,
        out_shape=jax.ShapeDtypeStruct((n, D_MODEL), F32),
        compiler_params=_cparams("arbitrary"),
        name="final",
    )(x1, moe, mod, final_norm)


def _mixer(x, mod, tokens_per_mod, seq, w, rope_tabs, cache=None):
    if rope_tabs is None:
        z, kv = _inproj(x, mod, w["norm_mix"], w["w_in"], tokens_per_mod)
        attn = _attn_ctx(z, w["sink"])
    else:
        z = _inproj(x, mod, w["norm_mix"], w["w_in"], tokens_per_mod, rope_tabs)
        kv = None
        attn = _attn_lat(z, cache[0], cache[1], w["sink"])
    fmix = _fourier(z, seq)
    mrg = _merge1(fmix, attn, z, w["w_fo"], w["w_ao"])
    x1, hp, aff_t = _merge2(mrg, x, mod, w["norm_ffn"], w["w_out"], w["w_rt"], tokens_per_mod)
    return kv, x1, hp, aff_t


def kernel(x_prompt, x_sample, cache_k, cache_v, c, c_ctx, w_mod, b_mod, norm_mix, w_in, attn_sink,
           w_fourier_out, w_attn_out, w_merge_out, norm_ffn, w_router, w_exp_gate, w_exp_up,
           w_exp_down, final_norm):
    l = 0
    cv = jnp.zeros((16, D_MODEL), F32).at[0].set(c_ctx).at[1:1 + DEC_BATCH].set(c)
    mod = _ada(cv, w_mod[l], b_mod[l])[:1 + DEC_BATCH].reshape(1 + DEC_BATCH, N_MOD, D_MODEL)
    mod_ctx, mod_lat = mod[:1], mod[1:]

    w = {
        "norm_mix": norm_mix[l].reshape(1, D_MODEL),
        "norm_ffn": norm_ffn[l].reshape(1, D_MODEL),
        "w_in": w_in[l].astype(BF16),
        "w_fo": w_fourier_out[l].astype(BF16),
        "w_ao": w_attn_out[l].astype(BF16),
        "w_out": w_merge_out[l].astype(BF16),
        "w_rt": w_router[l].T.astype(BF16),
        "sink": attn_sink[l].reshape(1, N_HEADS),
    }
    xp = x_prompt.reshape(N_CTX, D_MODEL)
    xs = x_sample.reshape(N_LAT, D_MODEL)
    ck = cache_k[:, l].reshape(DEC_BATCH, PAST_LEN, KV_DIM)
    cvv = cache_v[:, l].reshape(DEC_BATCH, PAST_LEN, KV_DIM)

    kv, x1_c, hp_c, aff_c = _mixer(xp, mod_ctx, N_CTX, SEQ, w, None)
    _, x1_l, hp_l, aff_l = _mixer(xs, mod_lat, DEC_SEQ, DEC_SEQ, w, _rope_tables(), (ck, cvv))

    idx_c, gate_c = _route(aff_c, CAP_CTX)
    idx_l, gate_l = _route(aff_l, CAP_LAT)
    hact = _moe1(idx_c, idx_l, hp_c, hp_l, w_exp_gate[l], w_exp_up[l])
    y_c, y_l = _moe2(hact, w_exp_down[l], gate_c, gate_l)
    moe_c = _combine(idx_c, y_c, N_CTX)
    moe_l = _combine(idx_l, y_l, N_LAT)

    fn = final_norm.reshape(1, D_MODEL)
    y_prompt = _final(x1_c, moe_c, mod_ctx, fn, N_CTX).reshape(BATCH, SEQ, D_MODEL)
    y_sample = _final(x1_l, moe_l, mod_lat, fn, DEC_SEQ).reshape(DEC_BATCH, DEC_SEQ, D_MODEL)
    state_k = kv[:, :KV_DIM].reshape(BATCH, 1, SEQ, N_KV_HEADS, HEAD_DIM)
    state_v = kv[:, KV_DIM:].reshape(BATCH, 1, SEQ, N_KV_HEADS, HEAD_DIM)
    return (y_prompt, y_sample, state_k, state_v)
```

```python
import functools
import math

import numpy as np
import jax
import jax.numpy as jnp
from jax import lax
from jax.experimental import pallas as pl
from jax.experimental.pallas import tpu as pltpu

D_MODEL = 2048
BATCH = 16
SEQ = 256
DEC_BATCH = 4
DEC_SEQ = 2048
PAST_LEN = 256
GRID_W = 64
N_HEADS = 16
N_KV_HEADS = 4
HEAD_DIM = 128
Q_GROUPS = N_HEADS // N_KV_HEADS
WINDOW = 128
ROPE_THETA = 10000.0
FOURIER_GROUPS = 4
FOURIER_GROUP_DIM = 256
FOURIER_DIM = FOURIER_GROUPS * FOURIER_GROUP_DIM
N_EXPERTS = 16
EXPERT_FF = 2048
CAPACITY_FACTOR = 2
Q_DIM = N_HEADS * HEAD_DIM
KV_DIM = N_KV_HEADS * HEAD_DIM
IN_DIM = FOURIER_DIM + Q_DIM + 2 * KV_DIM + 2 * D_MODEL
N_MOD = 6
EPS = 1e-6
NEG_INF = -1e30

F32 = jnp.float32
BF16 = jnp.bfloat16
LANES = 128
VMEM_LIMIT = 56 * 1024 * 1024
N_CTX = BATCH * SEQ
N_LAT = DEC_BATCH * DEC_SEQ
CAP_CTX = CAPACITY_FACTOR * N_CTX // N_EXPERTS
CAP_LAT = CAPACITY_FACTOR * N_LAT // N_EXPERTS
CAP_ALL = CAP_CTX + CAP_LAT


def _cparams(*sem, row_dmas=False):
    return pltpu.CompilerParams(dimension_semantics=sem, vmem_limit_bytes=VMEM_LIMIT,
                                disable_bounds_checks=row_dmas)


def _split_bf16(x):
    hi = x.astype(BF16)
    lo = (x - hi.astype(F32)).astype(BF16)
    return hi, lo


def _ada_body(cv_ref, w_ref, b_ref, o_ref):
    cv = cv_ref[...]
    s = cv * jax.nn.sigmoid(cv)
    s_hi, s_lo = _split_bf16(s)
    w_hi, w_lo = _split_bf16(w_ref[...])
    acc = jnp.dot(s_hi, w_hi, preferred_element_type=F32)
    acc += jnp.dot(s_hi, w_lo, preferred_element_type=F32)
    acc += jnp.dot(s_lo, w_hi, preferred_element_type=F32)
    o_ref[...] = acc + b_ref[...]


def _ada(cv, w_mod, b_mod):
    rows = cv.shape[0]
    tn = 1024
    return pl.pallas_call(
        _ada_body,
        grid=(N_MOD * D_MODEL // tn,),
        in_specs=[pl.BlockSpec((rows, D_MODEL), lambda j: (0, 0)),
                  pl.BlockSpec((D_MODEL, tn), lambda j: (0, j)),
                  pl.BlockSpec((1, tn), lambda j: (0, j))],
        out_specs=pl.BlockSpec((rows, tn), lambda j: (0, j)),
        out_shape=jax.ShapeDtypeStruct((rows, N_MOD * D_MODEL), F32),
        compiler_params=_cparams("arbitrary"),
        name="ada",
    )(cv, w_mod, b_mod.reshape(1, -1))


def _rmsnorm_mod(x, gain, shift, scale):
    ms = jnp.mean(x * x, axis=-1, keepdims=True)
    y = x * lax.rsqrt(ms + EPS) * gain
    return y * (1.0 + scale) + shift


def _rope_chunk(c, cos, sin_signed, low):
    partner = jnp.where(low, pltpu.roll(c, 96, 1), pltpu.roll(c, 32, 1))
    return c * cos + partner * sin_signed


def _inproj_body(*refs, rope, tn):
    if rope:
        x_ref, mod_ref, gn_ref, w_ref, cos_ref, sin_ref, z_ref, h_scr = refs
    else:
        x_ref, mod_ref, gn_ref, w_ref, z_ref, kv_ref, h_scr = refs
    j = pl.program_id(1)

    @pl.when(j == 0)
    def _():
        h = _rmsnorm_mod(x_ref[...], gn_ref[...], mod_ref[0, 0:1, :], mod_ref[0, 1:2, :])
        h_scr[...] = h.astype(BF16)

    acc = jnp.dot(h_scr[...], w_ref[...], preferred_element_type=F32)
    heads_per_tile = tn // HEAD_DIM
    q_lo, q_hi = FOURIER_DIM // tn, (FOURIER_DIM + Q_DIM) // tn

    if not rope:
        z_ref[...] = acc.astype(BF16)

        @pl.when(j == q_hi)
        def _():
            kv_ref[...] = acc
    else:
        def roped(nheads):
            cos = cos_ref[...]
            sin = sin_ref[...]
            low = (lax.broadcasted_iota(jnp.int32, cos.shape, 1) & 32) == 0
            for hd in range(nheads):
                sl = slice(hd * HEAD_DIM, (hd + 1) * HEAD_DIM)
                z_ref[:, sl] = _rope_chunk(acc[:, sl], cos, sin, low).astype(BF16)

        @pl.when((j < q_lo) | (j > q_hi))
        def _():
            z_ref[...] = acc.astype(BF16)

        @pl.when((j >= q_lo) & (j < q_hi))
        def _():
            roped(heads_per_tile)

        @pl.when(j == q_hi)
        def _():
            roped(N_KV_HEADS)
            z_ref[:, KV_DIM:] = acc[:, KV_DIM:].astype(BF16)


def _inproj(x, mod, gain, w_in_b, tokens_per_mod, rope_tabs=None):
    n = x.shape[0]
    tm, tn = 1024, 1024
    rope = rope_tabs is not None
    in_specs = [pl.BlockSpec((tm, D_MODEL), lambda i, j: (i, 0)),
                pl.BlockSpec((1, N_MOD, D_MODEL), lambda i, j: (i * tm // tokens_per_mod, 0, 0)),
                pl.BlockSpec((1, D_MODEL), lambda i, j: (0, 0)),
                pl.BlockSpec((D_MODEL, tn), lambda i, j: (0, j))]
    args = [x, mod, gain, w_in_b]
    z_spec = pl.BlockSpec((tm, tn), lambda i, j: (i, j))
    z_shape = jax.ShapeDtypeStruct((n, IN_DIM), BF16)
    if rope:
        tiles_per_seq = DEC_SEQ // tm
        in_specs += [pl.BlockSpec((tm, HEAD_DIM), lambda i, j: (i % tiles_per_seq, 0))] * 2
        args += list(rope_tabs)
        out_specs, out_shape = z_spec, z_shape
    else:
        out_specs = [z_spec, pl.BlockSpec((tm, 2 * KV_DIM), lambda i, j: (i, 0))]
        out_shape = [z_shape, jax.ShapeDtypeStruct((n, 2 * KV_DIM), F32)]
    return pl.pallas_call(
        functools.partial(_inproj_body, rope=rope, tn=tn),
        grid=(n // tm, IN_DIM // tn),
        in_specs=in_specs, out_specs=out_specs, out_shape=out_shape,
        scratch_shapes=[pltpu.VMEM((tm, D_MODEL), BF16)],
        compiler_params=_cparams("arbitrary", "arbitrary"),
        name="inproj_lat" if rope else "inproj_ctx",
    )(*args)


def _rope_tables():
    d = np.arange(HEAD_DIM)
    half, dd = d // 64, d % 64
    inv = ROPE_THETA ** (-jnp.arange(0, 64, 2, dtype=F32) / 64)
    inv_d = inv[jnp.asarray(dd % 32)]
    s = np.arange(DEC_SEQ)
    pos = np.where(half[None, :] == 0, (s // GRID_W)[:, None], (s % GRID_W)[:, None])
    ang = jnp.asarray(pos, F32) * inv_d[None, :]
    sign = jnp.asarray(np.where(dd < 32, -1.0, 1.0), F32)
    return jnp.cos(ang), jnp.sin(ang) * sign[None, :]


def _fourier_body(u_ref, cc_ref, cs_ref, o_ref, pq_scr, *, seq):
    @pl.when(pl.program_id(1) == 0)
    def _():
        for g in range(FOURIER_GROUPS):
            sl = slice(g * FOURIER_GROUP_DIM, (g + 1) * FOURIER_GROUP_DIM)
            pq = jnp.dot(u_ref[:, sl], cc_ref[...], preferred_element_type=F32)
            pq_scr[0:seq, sl] = pq[:, :FOURIER_GROUP_DIM].astype(BF16)
            pq_scr[seq:2 * seq, sl] = pq[:, FOURIER_GROUP_DIM:].astype(BF16)

    o_ref[...] = jnp.dot(cs_ref[...], pq_scr[...], preferred_element_type=F32).astype(BF16)


def _dft_mats(n):
    k = np.arange(n)
    ang = 2.0 * np.pi * ((k[:, None] * k[None, :]) % n) / n
    return np.cos(ang) / math.sqrt(n), np.sin(ang) / math.sqrt(n)


def _fourier(z, seq):
    n = z.shape[0]
    tr = min(seq, 512)
    cc, sc = _dft_mats(FOURIER_GROUP_DIM)
    cs, ss = _dft_mats(seq)
    chan = jnp.asarray(np.concatenate([cc, sc], axis=1), F32).astype(BF16)
    posm = jnp.asarray(np.concatenate([cs, -ss], axis=1), F32).astype(BF16)
    tiles = seq // tr
    return pl.pallas_call(
        functools.partial(_fourier_body, seq=seq),
        grid=(n // seq, tiles),
        in_specs=[pl.BlockSpec((seq, FOURIER_DIM), lambda b, i: (b, 0)),
                  pl.BlockSpec((FOURIER_GROUP_DIM, 2 * FOURIER_GROUP_DIM), lambda b, i: (0, 0)),
                  pl.BlockSpec((tr, 2 * seq), lambda b, i: (i, 0))],
        out_specs=pl.BlockSpec((tr, FOURIER_DIM), lambda b, i: (b * tiles + i, 0)),
        out_shape=jax.ShapeDtypeStruct((n, FOURIER_DIM), BF16),
        scratch_shapes=[pltpu.VMEM((2 * seq, FOURIER_DIM), BF16)],
        compiler_params=_cparams("arbitrary", "arbitrary"),
        name=f"fourier_{seq}",
    )(z, chan, posm)


_NT = (((1,), (1,)), ((), ()))
_SCALE = HEAD_DIM ** -0.5


def _attn_ctx_body(sink_ref, q_ref, k_ref, v_ref, o_ref):
    h = pl.program_id(1)
    k = k_ref[...]
    v = v_ref[...]
    for g in range(Q_GROUPS):
        sl = slice(g * HEAD_DIM, (g + 1) * HEAD_DIM)
        sk = sink_ref[0, h * Q_GROUPS + g]
        s = lax.dot_general(q_ref[:, sl], k, _NT, preferred_element_type=F32) * _SCALE
        m = jnp.maximum(jnp.max(s, axis=-1, keepdims=True), sk)
        p = jnp.exp(s - m)
        denom = jnp.sum(p, axis=-1, keepdims=True) + jnp.exp(sk - m)
        o = jnp.dot(p.astype(BF16), v, preferred_element_type=F32) / denom
        o_ref[:, sl] = o.astype(BF16)


def _attn_ctx(z, sink):
    n = z.shape[0]
    gw = Q_GROUPS * HEAD_DIM
    q0 = FOURIER_DIM // gw
    k0 = (FOURIER_DIM + Q_DIM) // HEAD_DIM
    v0 = k0 + N_KV_HEADS
    return pl.pallas_call(
        _attn_ctx_body,
        grid=(n // SEQ, N_KV_HEADS),
        in_specs=[pl.BlockSpec(memory_space=pltpu.SMEM),
                  pl.BlockSpec((SEQ, gw), lambda b, h: (b, q0 + h)),
                  pl.BlockSpec((SEQ, HEAD_DIM), lambda b, h: (b, k0 + h)),
                  pl.BlockSpec((SEQ, HEAD_DIM), lambda b, h: (b, v0 + h))],
        out_specs=pl.BlockSpec((SEQ, gw), lambda b, h: (b, h)),
        out_shape=jax.ShapeDtypeStruct((n, Q_DIM), BF16),
        compiler_params=_cparams("arbitrary", "arbitrary"),
        name="attn_ctx",
    )(sink, z, z, z)


ATT_TQ = 256
ATT_WIN = ATT_TQ + 2 * WINDOW


def _attn_lat_body(sink_ref, q_ref, k_ref, v_ref, ck_ref, cv_ref, o_ref):
    h = pl.program_id(1)
    nq = pl.program_id(2)
    start = jnp.clip(nq * ATT_TQ - WINDOW, 0, DEC_SEQ - ATT_WIN)
    start = pl.multiple_of(start, WINDOW)
    kl = k_ref[pl.ds(start, ATT_WIN), :]
    vl = v_ref[pl.ds(start, ATT_WIN), :]
    kc = ck_ref[0].astype(BF16)
    vc = cv_ref[0].astype(BF16)
    qpos = nq * ATT_TQ + lax.broadcasted_iota(jnp.int32, (ATT_TQ, ATT_WIN), 0)
    kpos = start + lax.broadcasted_iota(jnp.int32, (ATT_TQ, ATT_WIN), 1)
    valid = jnp.abs(qpos - kpos) <= WINDOW
    for g in range(Q_GROUPS):
        sl = slice(g * HEAD_DIM, (g + 1) * HEAD_DIM)
        sk = sink_ref[0, h * Q_GROUPS + g]
        q = q_ref[:, sl]
        s_loc = lax.dot_general(q, kl, _NT, preferred_element_type=F32) * _SCALE
        s_loc = jnp.where(valid, s_loc, NEG_INF)
        s_ctx = lax.dot_general(q, kc, _NT, preferred_element_type=F32) * _SCALE
        m = jnp.maximum(jnp.maximum(jnp.max(s_loc, axis=-1, keepdims=True),
                                    jnp.max(s_ctx, axis=-1, keepdims=True)), sk)
        p_loc = jnp.exp(s_loc - m)
        p_ctx = jnp.exp(s_ctx - m)
        denom = (jnp.sum(p_loc, axis=-1, keepdims=True) + jnp.sum(p_ctx, axis=-1, keepdims=True)
                 + jnp.exp(sk - m))
        o = (jnp.dot(p_loc.astype(BF16), vl, preferred_element_type=F32)
             + jnp.dot(p_ctx.astype(BF16), vc, preferred_element_type=F32)) / denom
        o_ref[:, sl] = o.astype(BF16)


def _attn_lat(z, cache_k, cache_v, sink):
    n = z.shape[0]
    gw = Q_GROUPS * HEAD_DIM
    q0 = FOURIER_DIM // gw
    k0 = (FOURIER_DIM + Q_DIM) // HEAD_DIM
    v0 = k0 + N_KV_HEADS
    tiles = DEC_SEQ // ATT_TQ
    return pl.pallas_call(
        _attn_lat_body,
        grid=(n // DEC_SEQ, N_KV_HEADS, tiles),
        in_specs=[pl.BlockSpec(memory_space=pltpu.SMEM),
                  pl.BlockSpec((ATT_TQ, gw), lambda b, h, t: (b * tiles + t, q0 + h)),
                  pl.BlockSpec((DEC_SEQ, HEAD_DIM), lambda b, h, t: (b, k0 + h)),
                  pl.BlockSpec((DEC_SEQ, HEAD_DIM), lambda b, h, t: (b, v0 + h)),
                  pl.BlockSpec((1, PAST_LEN, HEAD_DIM), lambda b, h, t: (b, 0, h)),
                  pl.BlockSpec((1, PAST_LEN, HEAD_DIM), lambda b, h, t: (b, 0, h))],
        out_specs=pl.BlockSpec((ATT_TQ, gw), lambda b, h, t: (b * tiles + t, h)),
        out_shape=jax.ShapeDtypeStruct((n, Q_DIM), BF16),
        compiler_params=_cparams("arbitrary", "arbitrary", "arbitrary"),
        name="attn_lat",
    )(sink, z, z, z, cache_k, cache_v)


def _merge1_body(f_ref, a_ref, gf_ref, ga_ref, wf_ref, wa_ref, o_ref):
    f = jnp.dot(f_ref[...], wf_ref[...], preferred_element_type=F32)
    a = jnp.dot(a_ref[...], wa_ref[...], preferred_element_type=F32)
    gf = jax.nn.sigmoid(gf_ref[...].astype(F32))
    ga = jax.nn.sigmoid(ga_ref[...].astype(F32))
    o_ref[...] = (gf * f + ga * a).astype(BF16)


def _merge1(fmix, attn, z, w_fo_b, w_ao_b):
    n = fmix.shape[0]
    tm = 512
    g0 = (FOURIER_DIM + Q_DIM + 2 * KV_DIM) // D_MODEL
    return pl.pallas_call(
        _merge1_body,
        grid=(n // tm,),
        in_specs=[pl.BlockSpec((tm, FOURIER_DIM), lambda i: (i, 0)),
                  pl.BlockSpec((tm, Q_DIM), lambda i: (i, 0)),
                  pl.BlockSpec((tm, D_MODEL), lambda i: (i, g0)),
                  pl.BlockSpec((tm, D_MODEL), lambda i: (i, g0 + 1)),
                  pl.BlockSpec((FOURIER_DIM, D_MODEL), lambda i: (0, 0)),
                  pl.BlockSpec((Q_DIM, D_MODEL), lambda i: (0, 0))],
        out_specs=pl.BlockSpec((tm, D_MODEL), lambda i: (i, 0)),
        out_shape=jax.ShapeDtypeStruct((n, D_MODEL), BF16),
        compiler_params=_cparams("arbitrary"),
        name="merge1",
    )(fmix, attn, z, z, w_fo_b, w_ao_b)


def _merge2_body(m_ref, x_ref, mod_ref, gn_ref, wo_ref, wr_ref, x1_ref, hp_ref, aff_ref):
    out = jnp.dot(m_ref[...], wo_ref[...], preferred_element_type=F32)
    x1 = x_ref[...] + mod_ref[0, 2:3, :] * out
    x1_ref[...] = x1
    h2 = _rmsnorm_mod(x1, gn_ref[...], mod_ref[0, 3:4, :], mod_ref[0, 4:5, :])
    hb = h2.astype(BF16)
    hp_ref[...] = hb.astype(F32)
    logits = lax.dot_general(wr_ref[...], hb, _NT, preferred_element_type=F32)
    mx = jnp.max(logits, axis=0, keepdims=True)
    ex = jnp.exp(logits - mx)
    aff_ref[...] = ex / jnp.sum(ex, axis=0, keepdims=True)


def _merge2(mrg, x, mod, gain, w_out_b, w_rt_b, tokens_per_mod):
    n = x.shape[0]
    tm = 512
    return pl.pallas_call(
        _merge2_body,
        grid=(n // tm,),
        in_specs=[pl.BlockSpec((tm, D_MODEL), lambda i: (i, 0)),
                  pl.BlockSpec((tm, D_MODEL), lambda i: (i, 0)),
                  pl.BlockSpec((1, N_MOD, D_MODEL), lambda i: (i * tm // tokens_per_mod, 0, 0)),
                  pl.BlockSpec((1, D_MODEL), lambda i: (0, 0)),
                  pl.BlockSpec((D_MODEL, D_MODEL), lambda i: (0, 0)),
                  pl.BlockSpec((N_EXPERTS, D_MODEL), lambda i: (0, 0))],
        out_specs=[pl.BlockSpec((tm, D_MODEL), lambda i: (i, 0)),
                   pl.BlockSpec((tm, D_MODEL), lambda i: (i, 0)),
                   pl.BlockSpec((N_EXPERTS, tm), lambda i: (0, i))],
        out_shape=[jax.ShapeDtypeStruct((n, D_MODEL), F32),
                   jax.ShapeDtypeStruct((n, D_MODEL), F32),
                   jax.ShapeDtypeStruct((N_EXPERTS, n), F32)],
        compiler_params=_cparams("arbitrary"),
        name="merge2",
    )(mrg, x, mod, gain, w_out_b, w_rt_b)


ROUTE_JB = 256
COL_TOK = 0
COL_GATE = 8


def _cumsum_lanes(dst_ref, mask, n):
    x = jnp.where(mask, 1.0, 0.0).astype(BF16)
    tri = (lax.broadcasted_iota(jnp.int32, (LANES, LANES), 0)
           <= lax.broadcasted_iota(jnp.int32, (LANES, LANES), 1)).astype(BF16)
    off = jnp.zeros((N_EXPERTS, 1), F32)
    for c in range(n // LANES):
        sl = slice(c * LANES, (c + 1) * LANES)
        pc = jnp.dot(x[:, sl], tri, preferred_element_type=F32) + off
        dst_ref[:, sl] = pc
        off = pc[:, LANES - 1:LANES]


def _route_body(aff_ref, res_ref, key_scr, m_scr, r_scr, *, n, cap):
    e = pl.program_id(0)

    @pl.when(e == 0)
    def _():
        a = aff_ref[...]
        def search(k, thr):
            cand = thr | lax.shift_left(jnp.int32(1), 30 - k)
            cnt = jnp.sum(jnp.where(a >= lax.bitcast_convert_type(cand, F32), 1.0, 0.0), axis=1, keepdims=True)
            return jnp.where(cnt >= cap, cand, thr)

        thr_bits = lax.fori_loop(0, 31, search, jnp.zeros((N_EXPERTS, 1), jnp.int32))
        thr = lax.bitcast_convert_type(thr_bits, F32)
        gt = a > thr
        eq = a == thr
        need = cap - jnp.sum(jnp.where(gt, 1.0, 0.0), axis=1, keepdims=True)
        _cumsum_lanes(key_scr, eq, n)
        sel = gt | (eq & (key_scr[...] <= need))
        _cumsum_lanes(key_scr, sel, n)
        key_scr[...] = jnp.where(sel, key_scr[...] - 1.0, -1.0)

        tok = lax.broadcasted_iota(jnp.int32, (8, n), 1)
        row = lax.broadcasted_iota(jnp.int32, (8, n), 0)
        m_scr[0:8, :] = jnp.where(row == 0, tok // LANES, jnp.where(row == 1, tok % LANES, 0)).astype(F32)
        a_hi = a.astype(BF16).astype(F32)
        a_mid = (a - a_hi).astype(BF16).astype(F32)
        a_lo = (a - a_hi - a_mid).astype(BF16).astype(F32)
        m_scr[COL_GATE:COL_GATE + N_EXPERTS, :] = a_hi
        m_scr[COL_GATE + N_EXPERTS:COL_GATE + 2 * N_EXPERTS, :] = a_mid
        m_scr[COL_GATE + 2 * N_EXPERTS:COL_GATE + 3 * N_EXPERTS, :] = a_lo
        m_scr[COL_GATE + 3 * N_EXPERTS:, :] = jnp.zeros((LANES - COL_GATE - 3 * N_EXPERTS, n), F32)
        for c in range(n // LANES):
            sl = slice(c * LANES, (c + 1) * LANES)
            r_scr[sl, :] = m_scr[:, sl].T.astype(BF16)

    key = key_scr[pl.ds(e, 1), :]
    for jb in range(cap // ROUTE_JB):
        slot = (lax.broadcasted_iota(jnp.int32, (ROUTE_JB, 1), 0) + jb * ROUTE_JB).astype(F32)
        onehot = jnp.where(key == slot, 1.0, 0.0).astype(BF16)
        res_ref[0, jb * ROUTE_JB:(jb + 1) * ROUTE_JB, :] = jnp.dot(onehot, r_scr[...], preferred_element_type=F32)


def _route(aff_t, cap):
    n = aff_t.shape[1]
    res = pl.pallas_call(
        functools.partial(_route_body, n=n, cap=cap),
        grid=(N_EXPERTS,),
        in_specs=[pl.BlockSpec((N_EXPERTS, n), lambda e: (0, 0))],
        out_specs=pl.BlockSpec((1, cap, LANES), lambda e: (e, 0, 0)),
        out_shape=jax.ShapeDtypeStruct((N_EXPERTS, cap, LANES), F32),
        scratch_shapes=[pltpu.VMEM((N_EXPERTS, n), F32),
                        pltpu.VMEM((LANES, n), F32),
                        pltpu.VMEM((n, LANES), BF16)],
        compiler_params=_cparams("arbitrary"),
        name=f"route_{n}",
    )(aff_t)
    idx = (res[:, :, COL_TOK] * LANES + res[:, :, COL_TOK + 1]).astype(jnp.int32)
    ar = jnp.arange(N_EXPERTS)
    gate = (res[ar, :, COL_GATE + ar] + res[ar, :, COL_GATE + N_EXPERTS + ar]
            + res[ar, :, COL_GATE + 2 * N_EXPERTS + ar])
    return idx, gate


MOE_TF = 512
MOE_STEPS = EXPERT_FF // MOE_TF
PIECE = CAP_ALL // MOE_STEPS


def _piece_parts(q):
    lo, hi = q * PIECE, (q + 1) * PIECE
    parts = []
    if lo < CAP_CTX:
        parts.append((0, lo, min(hi, CAP_CTX)))
    if hi > CAP_CTX:
        parts.append((1, max(lo, CAP_CTX) - CAP_CTX, hi - CAP_CTX))
    return parts


def _moe1_body(idxc_ref, idxl_ref, hc_ref, hl_ref, wg_ref, wu_ref, o_ref, stage, xb, sems):
    e = pl.program_id(0)
    f = pl.program_id(1)
    nxt = e + 1
    idx_refs, h_refs = (idxc_ref, idxl_ref), (hc_ref, hl_ref)

    def issue(expert, q, slot):
        for grp, first, last in _piece_parts(q):
            off = (CAP_CTX if grp else 0) - q * PIECE

            def body(j, carry, grp=grp, off=off):
                t = idx_refs[grp][expert, j]
                pltpu.make_async_copy(h_refs[grp].at[pl.ds(t, 1), :], stage.at[slot, pl.ds(j + off, 1), :],
                                      sems.at[slot]).start()
                return carry

            lax.fori_loop(first, last, body, 0, unroll=8)

    def consume(q, slot, dst):
        pltpu.make_async_copy(hl_ref.at[pl.ds(0, PIECE), :], stage.at[slot], sems.at[slot]).wait()
        xb[dst, q * PIECE:(q + 1) * PIECE, :] = stage[slot].astype(BF16)

    @pl.when((e == 0) & (f == 0))
    def _():
        for q in range(MOE_STEPS):
            issue(0, q, 0)
            consume(q, 0, 0)

    for q in range(MOE_STEPS):
        @pl.when(f == q)
        def _(q=q):
            if q == 0:
                @pl.when(e > 0)
                def _():
                    consume(MOE_STEPS - 1, (MOE_STEPS - 1) % 2, e % 2)
            else:
                @pl.when(nxt < N_EXPERTS)
                def _():
                    consume(q - 1, (q - 1) % 2, nxt % 2)

            @pl.when(nxt < N_EXPERTS)
            def _():
                issue(nxt, q, q % 2)

    x = xb[e % 2]
    a = jnp.dot(x, wg_ref[0].astype(BF16), preferred_element_type=F32)
    u = jnp.dot(x, wu_ref[0].astype(BF16), preferred_element_type=F32)
    o_ref[0] = (a * jax.nn.sigmoid(a) * u).astype(BF16)


def _moe1(idx_c, idx_l, h_c, h_l, w_gate, w_up):
    wspec = pl.BlockSpec((1, D_MODEL, MOE_TF), lambda e, f, ic, il: (e, 0, f))
    grid_spec = pltpu.PrefetchScalarGridSpec(
        num_scalar_prefetch=2,
        grid=(N_EXPERTS, MOE_STEPS),
        in_specs=[pl.BlockSpec(memory_space=pl.ANY), pl.BlockSpec(memory_space=pl.ANY), wspec, wspec],
        out_specs=pl.BlockSpec((1, CAP_ALL, MOE_TF), lambda e, f, ic, il: (e, 0, f)),
        scratch_shapes=[pltpu.VMEM((2, PIECE, D_MODEL), F32),
                        pltpu.VMEM((2, CAP_ALL, D_MODEL), BF16),
                        pltpu.SemaphoreType.DMA((2,))])
    return pl.pallas_call(
        _moe1_body,
        grid_spec=grid_spec,
        out_shape=jax.ShapeDtypeStruct((N_EXPERTS, CAP_ALL, EXPERT_FF), BF16),
        compiler_params=_cparams("arbitrary", "arbitrary", row_dmas=True),
        name="moe1",
    )(idx_c, idx_l, h_c, h_l, w_gate, w_up)


def _moe2_body(h_ref, wd_ref, gc_ref, gl_ref, yc_ref, yl_ref):
    y = jnp.dot(h_ref[0], wd_ref[0].astype(BF16), preferred_element_type=F32)
    yc_ref[0] = y[:CAP_CTX] * gc_ref[0]
    yl_ref[0] = y[CAP_CTX:] * gl_ref[0]


def _moe2(hact, w_down, gate_c, gate_l):
    tn = 512
    return pl.pallas_call(
        _moe2_body,
        grid=(N_EXPERTS, D_MODEL // tn),
        in_specs=[pl.BlockSpec((1, CAP_ALL, EXPERT_FF), lambda e, t: (e, 0, 0)),
                  pl.BlockSpec((1, EXPERT_FF, tn), lambda e, t: (e, 0, t)),
                  pl.BlockSpec((1, CAP_CTX, 1), lambda e, t: (e, 0, 0)),
                  pl.BlockSpec((1, CAP_LAT, 1), lambda e, t: (e, 0, 0))],
        out_specs=[pl.BlockSpec((1, CAP_CTX, tn), lambda e, t: (e, 0, t)),
                   pl.BlockSpec((1, CAP_LAT, tn), lambda e, t: (e, 0, t))],
        out_shape=[jax.ShapeDtypeStruct((N_EXPERTS, CAP_CTX, D_MODEL), F32),
                   jax.ShapeDtypeStruct((N_EXPERTS, CAP_LAT, D_MODEL), F32)],
        compiler_params=_cparams("arbitrary", "arbitrary"),
        name="moe2",
    )(hact, w_down, gate_c[..., None], gate_l[..., None])


def _combine_body(idx_ref, y_ref, out_ref, buf, sem_r, sem_w, *, cap, n):
    e = pl.program_id(0)

    half = cap // 2

    def wait_half(sem, h):
        pltpu.make_async_copy(out_ref.at[pl.ds(0, half), :], buf.at[pl.ds(h * half, half), :], sem.at[h]).wait()

    @pl.when(e == 0)
    def _():
        buf[...] = jnp.zeros_like(buf)
        chunks = n // cap
        for c in range(chunks):
            pltpu.make_async_copy(buf, out_ref.at[pl.ds(c * cap, cap), :], sem_w.at[0]).start()
        for c in range(chunks):
            pltpu.make_async_copy(buf, out_ref.at[pl.ds(c * cap, cap), :], sem_w.at[0]).wait()

    def read_rows(h):
        def body(j, carry):
            pltpu.make_async_copy(out_ref.at[pl.ds(idx_ref[0, 0, j], 1), :], buf.at[pl.ds(j, 1), :],
                                  sem_r.at[h]).start()
            return carry
        lax.fori_loop(h * half, (h + 1) * half, body, 0, unroll=8)

    def write_rows(h):
        def body(j, carry):
            pltpu.make_async_copy(buf.at[pl.ds(j, 1), :], out_ref.at[pl.ds(idx_ref[0, 0, j], 1), :],
                                  sem_w.at[h]).start()
            return carry
        lax.fori_loop(h * half, (h + 1) * half, body, 0, unroll=8)

    read_rows(0)
    read_rows(1)
    for h in range(2):
        wait_half(sem_r, h)
        buf[h * half:(h + 1) * half, :] += y_ref[0, h * half:(h + 1) * half, :]
        write_rows(h)
    wait_half(sem_w, 0)
    wait_half(sem_w, 1)


def _combine(idx, y, n):
    cap = idx.shape[1]
    return pl.pallas_call(
        functools.partial(_combine_body, cap=cap, n=n),
        grid=(N_EXPERTS,),
        in_specs=[pl.BlockSpec((1, 1, cap), lambda e: (e, 0, 0), memory_space=pltpu.SMEM),
                  pl.BlockSpec((1, cap, D_MODEL), lambda e: (e, 0, 0))],
        out_specs=pl.BlockSpec(memory_space=pl.ANY),
        out_shape=jax.ShapeDtypeStruct((n, D_MODEL), F32),
        scratch_shapes=[pltpu.VMEM((cap, D_MODEL), F32), pltpu.SemaphoreType.DMA((2,)),
                        pltpu.SemaphoreType.DMA((2,))],
        compiler_params=_cparams("arbitrary", row_dmas=True),
        name=f"combine_{cap}",
    )(idx.reshape(N_EXPERTS, 1, cap), y)


def _final_body(x1_ref, moe_ref, mod_ref, fn_ref, o_ref):
    x = x1_ref[...] + mod_ref[0, 5:6, :] * moe_ref[...]
    ms = jnp.mean(x * x, axis=-1, keepdims=True)
    o_ref[...] = x * lax.rsqrt(ms + EPS) * fn_ref[...]


def _final(x1, moe, mod, final_norm, tokens_per_mod):
    n = x1.shape[0]
    tm = 512
    row = pl.BlockSpec((tm, D_MODEL), lambda i: (i, 0))
    return pl.pallas_call(
        _final_body,
        grid=(n // tm,),
        in_specs=[row, row,
                  pl.BlockSpec((1, N_MOD, D_MODEL), lambda i: (i * tm // tokens_per_mod, 0, 0)),
                  pl.BlockSpec((1, D_MODEL), lambda i: (0, 0))],
        out_specs=row,
        out_shape=jax.ShapeDtypeStruct((n, D_MODEL), F32),
        compiler_params=_cparams("arbitrary"),
        name="final",
    )(x1, moe, mod, final_norm)


def _mixer(x, mod, tokens_per_mod, seq, w, rope_tabs, cache=None):
    if rope_tabs is None:
        z, kv = _inproj(x, mod, w["norm_mix"], w["w_in"], tokens_per_mod)
        attn = _attn_ctx(z, w["sink"])
    else:
        z = _inproj(x, mod, w["norm_mix"], w["w_in"], tokens_per_mod, rope_tabs)
        kv = None
        attn = _attn_lat(z, cache[0], cache[1], w["sink"])
    fmix = _fourier(z, seq)
    mrg = _merge1(fmix, attn, z, w["w_fo"], w["w_ao"])
    x1, hp, aff_t = _merge2(mrg, x, mod, w["norm_ffn"], w["w_out"], w["w_rt"], tokens_per_mod)
    return kv, x1, hp, aff_t


def kernel(x_prompt, x_sample, cache_k, cache_v, c, c_ctx, w_mod, b_mod, norm_mix, w_in, attn_sink,
           w_fourier_out, w_attn_out, w_merge_out, norm_ffn, w_router, w_exp_gate, w_exp_up,
           w_exp_down, final_norm):
    l = 0
    cv = jnp.zeros((16, D_MODEL), F32).at[0].set(c_ctx).at[1:1 + DEC_BATCH].set(c)
    mod = _ada(cv, w_mod[l], b_mod[l])[:1 + DEC_BATCH].reshape(1 + DEC_BATCH, N_MOD, D_MODEL)
    mod_ctx, mod_lat = mod[:1], mod[1:]

    w = {
        "norm_mix": norm_mix[l].reshape(1, D_MODEL),
        "norm_ffn": norm_ffn[l].reshape(1, D_MODEL),
        "w_in": w_in[l].astype(BF16),
        "w_fo": w_fourier_out[l].astype(BF16),
        "w_ao": w_attn_out[l].astype(BF16),
        "w_out": w_merge_out[l].astype(BF16),
        "w_rt": w_router[l].T.astype(BF16),
        "sink": attn_sink[l].reshape(1, N_HEADS),
    }
    xp = x_prompt.reshape(N_CTX, D_MODEL)
    xs = x_sample.reshape(N_LAT, D_MODEL)
    ck = cache_k[:, l].reshape(DEC_BATCH, PAST_LEN, KV_DIM)
    cvv = cache_v[:, l].reshape(DEC_BATCH, PAST_LEN, KV_DIM)

    kv, x1_c, hp_c, aff_c = _mixer(xp, mod_ctx, N_CTX, SEQ, w, None)
    _, x1_l, hp_l, aff_l = _mixer(xs, mod_lat, DEC_SEQ, DEC_SEQ, w, _rope_tables(), (ck, cvv))

    idx_c, gate_c = _route(aff_c, CAP_CTX)
    idx_l, gate_l = _route(aff_l, CAP_LAT)
    hact = _moe1(idx_c, idx_l, hp_c, hp_l, w_exp_gate[l], w_exp_up[l])
    y_c, y_l = _moe2(hact, w_exp_down[l], gate_c, gate_l)
    moe_c = _combine(idx_c, y_c, N_CTX)
    moe_l = _combine(idx_l, y_l, N_LAT)

    fn = final_norm.reshape(1, D_MODEL)
    y_prompt = _final(x1_c, moe_c, mod_ctx, fn, N_CTX).reshape(BATCH, SEQ, D_MODEL)
    y_sample = _final(x1_l, moe_l, mod_lat, fn, DEC_SEQ).reshape(DEC_BATCH, DEC_SEQ, D_MODEL)
    state_k = kv[:, :KV_DIM].reshape(BATCH, 1, SEQ, N_KV_HEADS, HEAD_DIM)
    state_v = kv[:, KV_DIM:].reshape(BATCH, 1, SEQ, N_KV_HEADS, HEAD_DIM)
    return (y_prompt, y_sample, state_k, state_v)
```
